```python
import numpy as np
import jax
import jax.numpy as jnp
from jax import lax

D_MODEL = 4096
BATCH = 4
SEQ = 2048
DEPTH = 4
DEC_BATCH = 8
DEC_SEQ = 8
PAST_LEN = 8192
PAGE_SIZE = 128

DH = 128
D_A = D_MODEL // 4
H_A = D_A // DH
W_B = D_MODEL // 2
N_RG_BLOCKS = 16
RG_BLOCK = W_B // N_RG_BLOCKS
CONV_W = 4
RG_C = 8.0
D_C = D_MODEL // 4
H_C = D_C // DH
N_BRANCH = 3
N_IN = 3 * D_A + H_A + 2 * W_B + 3 * D_C + N_BRANCH * D_MODEL
N_GROUPS = 4
EXPERTS_PER_GROUP = 8
N_EXPERTS = N_GROUPS * EXPERTS_PER_GROUP
TOP_K = 2
D_EXPERT = D_MODEL // 8
Q_BLOCK = 128
LN_EPS = 1e-5
ALPHA = (2.0 * DEPTH) ** 0.25
BETA = (8.0 * DEPTH) ** -0.25
FORGET_BIAS_MEAN = 4.0
FORGET_W_SCALE = 0.1

kernel_name = "fox_rglru_stickbreak_gated_moe_decoder_step"


def _layernorm(x, g, b):
    xf = x.astype(jnp.float32)
    mu = jnp.mean(xf, axis=-1, keepdims=True)
    var = jnp.mean(jnp.square(xf - mu), axis=-1, keepdims=True)
    y = (xf - mu) * lax.rsqrt(var + LN_EPS) * g.astype(jnp.float32) + b.astype(jnp.float32)
    return y.astype(x.dtype)


def _split_in(z):
    sizes = (D_A, D_A, D_A, H_A, W_B, W_B, D_C, D_C, D_C, N_BRANCH * D_MODEL)
    cuts = np.cumsum(sizes)[:-1].tolist()
    return jnp.split(z, cuts, axis=-1)


def _gather_pages(cache_l, page_table):
    g = cache_l[page_table]
    return g.reshape((g.shape[0], g.shape[1] * g.shape[2]) + g.shape[3:])


def _fox_block(q, k, v, cq, ck, qpos, kpos):
    scale = q.shape[-1] ** -0.5
    s = jnp.einsum('bqhd,bkhd->bhqk', q, k).astype(jnp.float32) * scale
    s = s + jnp.transpose(cq, (0, 2, 1))[..., None] - jnp.transpose(ck, (0, 2, 1))[:, :, None, :]
    mask = kpos[None, :] <= qpos[:, None]
    s = jnp.where(mask, s, -jnp.inf)
    p = jax.nn.softmax(s, axis=-1)
    return jnp.einsum('bhqk,bkhd->bqhd', p.astype(v.dtype), v)


def _sb_block(q, k, v, qpos, kpos):
    scale = q.shape[-1] ** -0.5
    z = jnp.einsum('bqhd,bkhd->bhqk', q, k).astype(jnp.float32) * scale
    mask = kpos[None, :] < qpos[:, None]
    log_1mb = jnp.where(mask, jax.nn.log_sigmoid(-z), 0.0)
    after = lax.cumsum(log_1mb, axis=3, reverse=True) - log_1mb
    a = jnp.where(mask, jnp.exp(jax.nn.log_sigmoid(z) + after), 0.0)
    return jnp.einsum('bhqk,bkhd->bqhd', a.astype(v.dtype), v)


def _fox_prompt(q, k, v, lf):
    b, s, h, dh = q.shape
    nb = s // Q_BLOCK
    c = jnp.cumsum(lf, axis=1)
    pos = jnp.arange(s)
    qb = q.reshape(b, nb, Q_BLOCK, h, dh).swapaxes(0, 1)
    cb = c.reshape(b, nb, Q_BLOCK, h).swapaxes(0, 1)
    pb = pos.reshape(nb, Q_BLOCK)
    ob = lax.map(lambda a: _fox_block(a[0], k, v, a[1], c, a[2], pos), (qb, cb, pb))
    return ob.swapaxes(0, 1).reshape(b, s, h * dh)


def _sb_prompt(q, k, v):
    b, s, h, dh = q.shape
    nb = s // Q_BLOCK
    pos = jnp.arange(s)
    qb = q.reshape(b, nb, Q_BLOCK, h, dh).swapaxes(0, 1)
    pb = pos.reshape(nb, Q_BLOCK)
    ob = lax.map(lambda a: _sb_block(a[0], k, v, a[1], pos), (qb, pb))
    return ob.swapaxes(0, 1).reshape(b, s, h * dh)


def _fox_sample(q, k, v, lf, pk, pv, plf):
    b, t, h, dh = q.shape
    p = pk.shape[1]
    kk = jnp.concatenate([pk.astype(k.dtype), k], axis=1)
    vv = jnp.concatenate([pv.astype(v.dtype), v], axis=1)
    c = jnp.cumsum(jnp.concatenate([plf.astype(jnp.float32), lf], axis=1), axis=1)
    o = _fox_block(q, kk, vv, c[:, p:], c, p + jnp.arange(t), jnp.arange(p + t))
    return o.reshape(b, t, h * dh)


def _sb_sample(q, k, v, pk, pv):
    b, t, h, dh = q.shape
    p = pk.shape[1]
    kk = jnp.concatenate([pk.astype(k.dtype), k], axis=1)
    vv = jnp.concatenate([pv.astype(v.dtype), v], axis=1)
    o = _sb_block(q, kk, vv, p + jnp.arange(t), jnp.arange(p + t))
    return o.reshape(b, t, h * dh)


def _lin_combine(e1, e2):
    a1, b1 = e1
    a2, b2 = e2
    return a1 * a2, a2 * b1 + b2


def _rglru_branch(u, g, conv_w, conv_b, w_rg_a, b_rg_a, w_rg_x, b_rg_x, lam, h0, buf):
    b, t, w = u.shape
    xx = jnp.concatenate([buf.astype(u.dtype), u], axis=1)
    xc = conv_b + sum(xx[:, k:k + t] * conv_w[k] for k in range(CONV_W))
    new_buf = xx[:, t:]
    xb = xc.reshape(b, t, N_RG_BLOCKS, RG_BLOCK)
    r = jax.nn.sigmoid((jnp.einsum('btnc,ncd->btnd', xb, w_rg_a) + b_rg_a).astype(jnp.float32)).reshape(b, t, w)
    i = jax.nn.sigmoid((jnp.einsum('btnc,ncd->btnd', xb, w_rg_x) + b_rg_x).astype(jnp.float32)).reshape(b, t, w)
    log_a = -RG_C * r * jax.nn.softplus(-lam.astype(jnp.float32))
    a = jnp.exp(log_a)
    mult = jnp.sqrt(-jnp.expm1(2.0 * log_a))
    bt = mult * (i * xc.astype(jnp.float32))
    bt = bt.at[:, 0].add(a[:, 0] * h0.astype(jnp.float32))
    _, h = lax.associative_scan(_lin_combine, (a, bt), axis=1)
    y = (h * jax.nn.gelu(g.astype(jnp.float32))).astype(u.dtype)
    return y, h[:, -1].astype(h0.dtype), new_buf


def _moe(x, w_router, b_router, w_e_gate, w_e_up, w_e_down):
    b, t, d = x.shape
    xt = x.reshape(b * t, d)
    n = xt.shape[0]
    logits = (xt @ w_router).astype(jnp.float32) + b_router.astype(jnp.float32)
    probs = jax.nn.softmax(logits, axis=-1)
    grouped = probs.reshape(n, N_GROUPS, EXPERTS_PER_GROUP)
    group_score = lax.top_k(grouped, TOP_K)[0].sum(-1)
    g_sel = jnp.argmax(group_score, axis=-1)
    in_group = grouped[jnp.arange(n), g_sel]
    vals, idx = lax.top_k(in_group, TOP_K)
    wts = vals / jnp.sum(vals, axis=-1, keepdims=True)
    eid = g_sel[:, None] * EXPERTS_PER_GROUP + idx
    gates = jnp.sum(jax.nn.one_hot(eid, N_EXPERTS, dtype=jnp.float32) * wts[..., None], axis=1)
    hid = jax.nn.silu(jnp.einsum('nd,edf->nef', xt, w_e_gate)) * jnp.einsum('nd,edf->nef', xt, w_e_up)
    y = jnp.einsum('nef,efd->nd', hid * gates[:, :, None].astype(hid.dtype), w_e_down)
    return y.reshape(b, t, d)


def _layer(x, lp, w_router, b_router, past, h0, buf):
    b, t, _ = x.shape
    z = x @ lp['w_in']
    qa, ka, va, fa, ub, gb, qc, kc, vc, gl = _split_in(z)
    qa = qa.reshape(b, t, H_A, DH)
    ka = ka.reshape(b, t, H_A, DH)
    va = va.reshape(b, t, H_A, DH)
    qc = qc.reshape(b, t, H_C, DH)
    kc = kc.reshape(b, t, H_C, DH)
    vc = vc.reshape(b, t, H_C, DH)
    lf = jax.nn.log_sigmoid(fa.astype(jnp.float32) + lp['b_f'].astype(jnp.float32))
    if past is None:
        ya = _fox_prompt(qa, ka, va, lf)
        yc = _sb_prompt(qc, kc, vc)
    else:
        pk_a, pv_a, plf_a, pk_c, pv_c = past
        ya = _fox_sample(qa, ka, va, lf, pk_a, pv_a, plf_a)
        yc = _sb_sample(qc, kc, vc, pk_c, pv_c)
    yb, h_last, new_buf = _rglru_branch(ub, gb, lp['conv_w'], lp['conv_b'], lp['w_rg_a'], lp['b_rg_a'],
                                        lp['w_rg_x'], lp['b_rg_x'], lp['lam'], h0, buf)
    gates = jax.nn.sigmoid(gl.astype(jnp.float32)).astype(x.dtype).reshape(b, t, N_BRANCH, D_MODEL)
    merged = (gates[:, :, 0] * (ya @ lp['w_proj_a'])
              + gates[:, :, 1] * (yb @ lp['w_proj_b'])
              + gates[:, :, 2] * (yc @ lp['w_proj_c']))
    x = _layernorm(ALPHA * x + merged @ lp['w_out'], lp['ln1_g'], lp['ln1_b'])
    x = _layernorm(ALPHA * x + _moe(x, w_router, b_router, lp['w_e_gate'], lp['w_e_up'], lp['w_e_down']),
                   lp['ln2_g'], lp['ln2_b'])
    return x, (ka, va, lf.astype(x.dtype), kc, vc, h_last, new_buf)


def setup_inputs(seed: int = 0) -> dict:
    key = jax.random.key(seed)
    ks = jax.random.split(key, 40)
    f32 = jnp.float32
    n_pages = PAST_LEN // PAGE_SIZE
    n_used = DEC_BATCH * n_pages
    n_pool = n_used + max(1, n_used // 4)

    def nrm(k, shape, scale):
        return jax.random.normal(k, shape, f32) * scale

    x_prompt = nrm(ks[0], (BATCH, SEQ, D_MODEL), 1.0)
    x_sample = nrm(ks[1], (DEC_BATCH, DEC_SEQ, D_MODEL), 1.0)
    cache_k_a = nrm(ks[2], (DEPTH, n_pool, PAGE_SIZE, H_A, DH), 1.0)
    cache_v_a = nrm(ks[3], (DEPTH, n_pool, PAGE_SIZE, H_A, DH), 1.0)
    cache_lf_a = jax.nn.log_sigmoid(FORGET_BIAS_MEAN + nrm(ks[4], (DEPTH, n_pool, PAGE_SIZE, H_A), 1.0))
    cache_k_c = nrm(ks[5], (DEPTH, n_pool, PAGE_SIZE, H_C, DH), 1.0)
    cache_v_c = nrm(ks[6], (DEPTH, n_pool, PAGE_SIZE, H_C, DH), 1.0)
    state_h_b = nrm(ks[7], (DEPTH, DEC_BATCH, W_B), 0.5)
    state_conv_b = nrm(ks[8], (DEPTH, DEC_BATCH, CONV_W - 1, W_B), 1.0)
    page_table = jax.random.permutation(ks[9], n_pool)[:n_used].reshape(DEC_BATCH, n_pages).astype(jnp.int32)

    col_scale = np.ones((N_IN,), np.float32)
    col_scale[3 * D_A:3 * D_A + H_A] = FORGET_W_SCALE
    w_in = nrm(ks[10], (DEPTH, D_MODEL, N_IN), D_MODEL ** -0.5) * jnp.asarray(col_scale)
    b_f = FORGET_BIAS_MEAN + nrm(ks[11], (DEPTH, H_A), 0.5)
    conv_w = nrm(ks[12], (DEPTH, CONV_W, W_B), CONV_W ** -0.5)
    conv_b = nrm(ks[13], (DEPTH, W_B), 0.01)
    w_rg_a = nrm(ks[14], (DEPTH, N_RG_BLOCKS, RG_BLOCK, RG_BLOCK), RG_BLOCK ** -0.5)
    b_rg_a = nrm(ks[15], (DEPTH, N_RG_BLOCKS, RG_BLOCK), 0.01)
    w_rg_x = nrm(ks[16], (DEPTH, N_RG_BLOCKS, RG_BLOCK, RG_BLOCK), RG_BLOCK ** -0.5)
    b_rg_x = nrm(ks[17], (DEPTH, N_RG_BLOCKS, RG_BLOCK), 0.01)
    a_c = jax.random.uniform(ks[18], (DEPTH, W_B), f32, 0.9, 0.999)
    a0 = a_c ** (1.0 / RG_C)
    lam = jnp.log(a0) - jnp.log1p(-a0)
    w_proj_a = nrm(ks[19], (DEPTH, D_A, D_MODEL), D_A ** -0.5 * BETA)
    w_proj_b = nrm(ks[20], (DEPTH, W_B, D_MODEL), W_B ** -0.5 * BETA)
    w_proj_c = nrm(ks[21], (DEPTH, D_C, D_MODEL), D_C ** -0.5 * BETA)
    w_out = nrm(ks[22], (DEPTH, D_MODEL, D_MODEL), D_MODEL ** -0.5 * BETA)
    ln1_g = 1.0 + nrm(ks[23], (DEPTH, D_MODEL), 0.01)
    ln1_b = nrm(ks[24], (DEPTH, D_MODEL), 0.01)
    w_router = nrm(ks[25], (D_MODEL, N_EXPERTS), D_MODEL ** -0.5)
    b_router = nrm(ks[26], (N_EXPERTS,), 0.01)
    w_e_gate = nrm(ks[27], (DEPTH, N_EXPERTS, D_MODEL, D_EXPERT), D_MODEL ** -0.5)
    w_e_up = nrm(ks[28], (DEPTH, N_EXPERTS, D_MODEL, D_EXPERT), D_MODEL ** -0.5)
    w_e_down = nrm(ks[29], (DEPTH, N_EXPERTS, D_EXPERT, D_MODEL), D_EXPERT ** -0.5 * BETA)
    ln2_g = 1.0 + nrm(ks[30], (DEPTH, D_MODEL), 0.01)
    ln2_b = nrm(ks[31], (DEPTH, D_MODEL), 0.01)
    return {"x_prompt": x_prompt, "x_sample": x_sample,
            "cache_k_a": cache_k_a, "cache_v_a": cache_v_a, "cache_lf_a": cache_lf_a,
            "cache_k_c": cache_k_c, "cache_v_c": cache_v_c,
            "state_h_b": state_h_b, "state_conv_b": state_conv_b, "page_table": page_table,
            "w_in": w_in, "b_f": b_f, "conv_w": conv_w, "conv_b": conv_b,
            "w_rg_a": w_rg_a, "b_rg_a": b_rg_a, "w_rg_x": w_rg_x, "b_rg_x": b_rg_x, "lam": lam,
            "w_proj_a": w_proj_a, "w_proj_b": w_proj_b, "w_proj_c": w_proj_c, "w_out": w_out,
            "ln1_g": ln1_g, "ln1_b": ln1_b, "w_router": w_router, "b_router": b_router,
            "w_e_gate": w_e_gate, "w_e_up": w_e_up, "w_e_down": w_e_down,
            "ln2_g": ln2_g, "ln2_b": ln2_b}


def reference(x_prompt, x_sample, cache_k_a, cache_v_a, cache_lf_a, cache_k_c, cache_v_c,
              state_h_b, state_conv_b, page_table, w_in, b_f, conv_w, conv_b,
              w_rg_a, b_rg_a, w_rg_x, b_rg_x, lam, w_proj_a, w_proj_b, w_proj_c, w_out,
              ln1_g, ln1_b, w_router, b_router, w_e_gate, w_e_up, w_e_down, ln2_g, ln2_b):
    xp, xs = x_prompt, x_sample
    bp = xp.shape[0]
    prompt_new = [[] for _ in range(7)]
    sample_new = [[] for _ in range(7)]
    for l in range(DEPTH):
        lp = {'w_in': w_in[l], 'b_f': b_f[l], 'conv_w': conv_w[l], 'conv_b': conv_b[l],
              'w_rg_a': w_rg_a[l], 'b_rg_a': b_rg_a[l], 'w_rg_x': w_rg_x[l], 'b_rg_x': b_rg_x[l],
              'lam': lam[l], 'w_proj_a': w_proj_a[l], 'w_proj_b': w_proj_b[l], 'w_proj_c': w_proj_c[l],
              'w_out': w_out[l], 'ln1_g': ln1_g[l], 'ln1_b': ln1_b[l], 'w_e_gate': w_e_gate[l],
              'w_e_up': w_e_up[l], 'w_e_down': w_e_down[l], 'ln2_g': ln2_g[l], 'ln2_b': ln2_b[l]}
        past = (_gather_pages(cache_k_a[l], page_table), _gather_pages(cache_v_a[l], page_table),
                _gather_pages(cache_lf_a[l], page_table), _gather_pages(cache_k_c[l], page_table),
                _gather_pages(cache_v_c[l], page_table))
        xp, st_p = _layer(xp, lp, w_router, b_router, None,
                          jnp.zeros((bp, W_B), xp.dtype), jnp.zeros((bp, CONV_W - 1, W_B), xp.dtype))
        xs, st_s = _layer(xs, lp, w_router, b_router, past, state_h_b[l], state_conv_b[l])
        for lst, s in zip(prompt_new, st_p):
            lst.append(s)
        for lst, s in zip(sample_new, st_s):
            lst.append(s)
    k_a_p, v_a_p, lf_a_p, k_c_p, v_c_p, h_b_p, conv_b_p = [jnp.stack(s) for s in prompt_new]
    k_a_s, v_a_s, lf_a_s, k_c_s, v_c_s, h_b_s, conv_b_s = [jnp.stack(s) for s in sample_new]
    return (xp, xs, k_a_p, v_a_p, lf_a_p, k_c_p, v_c_p, h_b_p, conv_b_p,
            k_a_s, v_a_s, lf_a_s, k_c_s, v_c_s, h_b_s, conv_b_s)
```

```python
import functools

import jax
import jax.numpy as jnp
from jax import lax
from jax.experimental import pallas as pl
from jax.experimental.pallas import tpu as pltpu

F32 = jnp.float32
BF16 = jnp.bfloat16

LANES = 128
SUBLANES = 8
VMEM_LIMIT_BYTES = 56 * 1024 * 1024

HEAD_DIM = 128
CONV_WIDTH = 4
RG_C = 8.0
N_GROUPS = 4
TOP_K = 2
LN_EPS = 1e-5
NEG_BIG = -1e30

ROW_ALIGN = 256
MM_TM = 768
MM_TN = 512
ATT_BLK = 256
RG_TT = 256
RG_CW = 512
MOE_TM = 256
TOK_TM = 256


def _cparams(sem):
    return pltpu.CompilerParams(dimension_semantics=sem, vmem_limit_bytes=VMEM_LIMIT_BYTES)


def _tile(n, pref, align):
    if n <= pref:
        return n
    t = (pref // align) * align
    while t >= align:
        if n % t == 0:
            return t
        t -= align
    raise ValueError(f"no tile for {n} (pref {pref}, align {align})")


def _log_sigmoid(x):
    return jnp.minimum(x, 0.0) - jnp.log1p(jnp.exp(-jnp.abs(x)))


def _split_dot(a, t_bf16, parts):
    acc = None
    r = a
    for p in range(parts):
        piece = r.astype(BF16)
        d = jnp.dot(piece, t_bf16, preferred_element_type=F32)
        acc = d if acc is None else acc + d
        if p + 1 < parts:
            r = r - piece.astype(F32)
    return acc


def _mm_body(*refs, epilogue, alpha):
    x_ref, w_ref = refs[0], refs[1]
    rest = refs[2:]
    acc = jnp.dot(x_ref[...], w_ref[...].astype(BF16), preferred_element_type=F32)
    if epilogue == "sigmoid":
        acc = jax.nn.sigmoid(acc)
    elif epilogue == "logsig_bias":
        acc = _log_sigmoid(acc + rest[0][...])
        rest = rest[1:]
    elif epilogue == "residual":
        acc = alpha * rest[0][...] + acc
        rest = rest[1:]
    for o_ref in rest:
        o_ref[...] = acc.astype(o_ref.dtype)


def _matmul(x, w, layer, col_blk0, n_cols, out_dtypes, *, tn, epilogue="none", extra=None,
            alpha=1.0, name):
    m, k = x.shape
    tm = _tile(m, MM_TM, ROW_ALIGN)
    grid = (m // tm, n_cols // tn)
    in_specs = [
        pl.BlockSpec((tm, k), lambda i, j: (i, 0)),
        pl.BlockSpec((None, k, tn), lambda i, j: (layer, 0, j + col_blk0)),
    ]
    args = [x, w]
    if epilogue == "logsig_bias":
        in_specs.append(pl.BlockSpec((None, 1, tn), lambda i, j: (layer, 0, j)))
        args.append(extra)
    elif epilogue == "residual":
        in_specs.append(pl.BlockSpec((tm, tn), lambda i, j: (i, j)))
        args.append(extra)
    outs = tuple(jax.ShapeDtypeStruct((m, n_cols), dt) for dt in out_dtypes)
    out_specs = tuple(pl.BlockSpec((tm, tn), lambda i, j: (i, j)) for _ in out_dtypes)
    return pl.pallas_call(
        functools.partial(_mm_body, epilogue=epilogue, alpha=alpha),
        grid=grid, in_specs=in_specs, out_specs=out_specs, out_shape=outs,
        compiler_params=_cparams(("parallel", "arbitrary")), name=name)(*args)


def _merge_body(ya, yb, yc, wa, wb, wc, g0, g1, g2, o_ref):
    pa = jnp.dot(ya[...], wa[...].astype(BF16), preferred_element_type=F32)
    pb = jnp.dot(yb[...], wb[...].astype(BF16), preferred_element_type=F32)
    pc = jnp.dot(yc[...], wc[...].astype(BF16), preferred_element_type=F32)
    m = g0[...].astype(F32) * pa + g1[...].astype(F32) * pb + g2[...].astype(F32) * pc
    o_ref[...] = m.astype(o_ref.dtype)


def _merge(ya, yb, yc, w_proj_a, w_proj_b, w_proj_c, gates, layer):
    m = ya.shape[0]
    d = w_proj_a.shape[-1]
    tm = _tile(m, MM_TM, ROW_ALIGN)
    tn = MM_TN
    nb = d // tn

    def act(width):
        return pl.BlockSpec((tm, width), lambda i, j: (i, 0))

    def wgt(width):
        return pl.BlockSpec((None, width, tn), lambda i, j: (layer, 0, j))

    def gate(branch):
        return pl.BlockSpec((tm, tn), lambda i, j: (i, j + branch * nb))

    return pl.pallas_call(
        _merge_body, grid=(m // tm, nb),
        in_specs=[act(ya.shape[1]), act(yb.shape[1]), act(yc.shape[1]),
                  wgt(ya.shape[1]), wgt(yb.shape[1]), wgt(yc.shape[1]),
                  gate(0), gate(1), gate(2)],
        out_specs=pl.BlockSpec((tm, tn), lambda i, j: (i, j)),
        out_shape=jax.ShapeDtypeStruct((m, d), BF16),
        compiler_params=_cparams(("parallel", "arbitrary")), name="merge")(
            ya, yb, yc, w_proj_a, w_proj_b, w_proj_c, gates, gates, gates)


def _layernorm_rows(x, g, b):
    mu = jnp.mean(x, axis=-1, keepdims=True)
    xc = x - mu
    var = jnp.mean(xc * xc, axis=-1, keepdims=True)
    return xc * lax.rsqrt(var + LN_EPS) * g + b


def _ln_body(x_ref, g_ref, b_ref, o_ref):
    o_ref[...] = _layernorm_rows(x_ref[...], g_ref[...], b_ref[...])


def _layernorm(x, g, b, layer):
    m, d = x.shape
    tm = _tile(m, TOK_TM, ROW_ALIGN)
    vec = pl.BlockSpec((None, 1, d), lambda i: (layer, 0, 0))
    row = pl.BlockSpec((tm, d), lambda i: (i, 0))
    return pl.pallas_call(
        _ln_body, grid=(m // tm,), in_specs=[row, vec, vec], out_specs=row,
        out_shape=jax.ShapeDtypeStruct((m, d), F32),
        compiler_params=_cparams(("parallel",)), name="ln1")(x, g, b)


def _cumsum_body(lf_ref, c_ref, carry_ref):
    ts = lf_ref.shape[0]

    @pl.when(pl.program_id(1) == 0)
    def _():
        carry_ref[...] = jnp.zeros_like(carry_ref)

    r = lax.broadcasted_iota(jnp.int32, (ts, ts), 0)
    c = lax.broadcasted_iota(jnp.int32, (ts, ts), 1)
    lower = jnp.where(c <= r, 1.0, 0.0).astype(BF16)
    acc = None
    rem = lf_ref[...]
    for p in range(3):
        piece = rem.astype(BF16)
        dd = jnp.dot(lower, piece, preferred_element_type=F32)
        acc = dd if acc is None else acc + dd
        if p < 2:
            rem = rem - piece.astype(F32)
    out = acc + carry_ref[...]
    c_ref[...] = out
    carry_ref[...] = out[ts - 1:ts, :]


def _prompt_cumsum(lf, batch, seq):
    ts = _tile(seq, ATT_BLK, SUBLANES)
    nt = seq // ts
    spec = pl.BlockSpec((ts, LANES), lambda b, t: (b * nt + t, 0))
    return pl.pallas_call(
        _cumsum_body, grid=(batch, nt), in_specs=[spec], out_specs=spec,
        out_shape=jax.ShapeDtypeStruct((batch * seq, LANES), F32),
        scratch_shapes=[pltpu.VMEM((1, LANES), F32)],
        compiler_params=_cparams(("parallel", "arbitrary")), name="lf_cumsum")(lf)


def _fox_prompt_body(q_ref, k_ref, v_ref, ccol_ref, crow_ref, o_ref, acc_ref, m_ref, l_ref,
                     *, n_heads, scale):
    i = pl.program_id(1)
    j = pl.program_id(2)
    tq = q_ref.shape[0]
    tk = k_ref.shape[0]

    @pl.when(j == 0)
    def _():
        acc_ref[...] = jnp.zeros_like(acc_ref)
        m_ref[...] = jnp.full_like(m_ref, NEG_BIG)
        l_ref[...] = jnp.zeros_like(l_ref)

    @pl.when(j <= i)
    def _():
        qpos = i * tq + lax.broadcasted_iota(jnp.int32, (tq, tk), 0)
        kpos = j * tk + lax.broadcasted_iota(jnp.int32, (tq, tk), 1)
        mask = kpos <= qpos
        for h in range(n_heads):
            hs = slice(h * HEAD_DIM, (h + 1) * HEAD_DIM)
            s = lax.dot_general(q_ref[:, hs], k_ref[:, hs], (((1,), (1,)), ((), ())),
                                preferred_element_type=F32) * scale
            s = s + ccol_ref[:, h:h + 1] - crow_ref[h:h + 1, :]
            s = jnp.where(mask, s, NEG_BIG)
            m_prev = m_ref[h]
            m_new = jnp.maximum(m_prev, jnp.max(s, axis=-1, keepdims=True))
            p = jnp.exp(s - m_new)
            a = jnp.exp(m_prev - m_new)
            l_ref[h] = a * l_ref[h] + jnp.sum(p, axis=-1, keepdims=True)
            acc_ref[:, hs] = a * acc_ref[:, hs] + jnp.dot(
                p.astype(BF16), v_ref[:, hs], preferred_element_type=F32)
            m_ref[h] = m_new

    @pl.when(j == i)
    def _():
        for h in range(n_heads):
            hs = slice(h * HEAD_DIM, (h + 1) * HEAD_DIM)
            o_ref[:, hs] = (acc_ref[:, hs] / l_ref[h]).astype(o_ref.dtype)


def _fox_prompt(qarr, kvarr, c_col, c_row, batch, seq, width, q_col, k_col, v_col):
    n_heads = width // HEAD_DIM
    blk = _tile(seq, ATT_BLK, LANES)
    nb = seq // blk
    kernel = functools.partial(_fox_prompt_body, n_heads=n_heads, scale=HEAD_DIM ** -0.5)
    return pl.pallas_call(
        kernel, grid=(batch, nb, nb),
        in_specs=[
            pl.BlockSpec((blk, width), lambda b, i, j: (b * nb + i, q_col)),
            pl.BlockSpec((blk, width), lambda b, i, j: (b * nb + jnp.minimum(j, i), k_col)),
            pl.BlockSpec((blk, width), lambda b, i, j: (b * nb + jnp.minimum(j, i), v_col)),
            pl.BlockSpec((blk, LANES), lambda b, i, j: (b * nb + i, 0)),
            pl.BlockSpec((None, SUBLANES, blk), lambda b, i, j: (b, 0, jnp.minimum(j, i))),
        ],
        out_specs=pl.BlockSpec((blk, width), lambda b, i, j: (b * nb + i, 0)),
        out_shape=jax.ShapeDtypeStruct((batch * seq, width), BF16),
        scratch_shapes=[pltpu.VMEM((blk, width), F32),
                        pltpu.VMEM((n_heads, blk, 1), F32),
                        pltpu.VMEM((n_heads, blk, 1), F32)],
        compiler_params=_cparams(("parallel", "parallel", "arbitrary")),
        name="fox_prompt")(qarr, kvarr, kvarr, c_col, c_row)


def _sb_prompt_body(q_ref, k_ref, v_ref, o_ref, acc_ref, carry_ref, *, n_heads, scale):
    i = pl.program_id(1)
    jj = pl.program_id(2)
    tq = q_ref.shape[0]
    tk = k_ref.shape[0]

    @pl.when(jj == 0)
    def _():
        acc_ref[...] = jnp.zeros_like(acc_ref)
        carry_ref[...] = jnp.zeros_like(carry_ref)

    @pl.when(jj <= i)
    def _():
        j = i - jj
        qpos = i * tq + lax.broadcasted_iota(jnp.int32, (tq, tk), 0)
        kpos = j * tk + lax.broadcasted_iota(jnp.int32, (tq, tk), 1)
        mask = kpos < qpos
        r = lax.broadcasted_iota(jnp.int32, (tk, tk), 0)
        c = lax.broadcasted_iota(jnp.int32, (tk, tk), 1)
        later = jnp.where(r > c, 1.0, 0.0).astype(BF16)
        for h in range(n_heads):
            hs = slice(h * HEAD_DIM, (h + 1) * HEAD_DIM)
            z = lax.dot_general(q_ref[:, hs], k_ref[:, hs], (((1,), (1,)), ((), ())),
                                preferred_element_type=F32) * scale
            log_1mb = _log_sigmoid(-z)
            lm = jnp.where(mask, log_1mb, 0.0)
            after = _split_dot(lm, later, 2) + carry_ref[h]
            a = jnp.where(mask, jnp.exp(z + log_1mb + after), 0.0)
            carry_ref[h] = carry_ref[h] + jnp.sum(lm, axis=-1, keepdims=True)
            acc_ref[:, hs] = acc_ref[:, hs] + jnp.dot(
                a.astype(BF16), v_ref[:, hs], preferred_element_type=F32)

    @pl.when(jj == i)
    def _():
        o_ref[...] = acc_ref[...].astype(o_ref.dtype)


def _sb_prompt(qarr, kvarr, batch, seq, width, q_col, k_col, v_col):
    n_heads = width // HEAD_DIM
    blk = _tile(seq, ATT_BLK, LANES)
    nb = seq // blk
    kernel = functools.partial(_sb_prompt_body, n_heads=n_heads, scale=HEAD_DIM ** -0.5)
    return pl.pallas_call(
        kernel, grid=(batch, nb, nb),
        in_specs=[
            pl.BlockSpec((blk, width), lambda b, i, jj: (b * nb + i, q_col)),
            pl.BlockSpec((blk, width), lambda b, i, jj: (b * nb + jnp.maximum(i - jj, 0), k_col)),
            pl.BlockSpec((blk, width), lambda b, i, jj: (b * nb + jnp.maximum(i - jj, 0), v_col)),
        ],
        out_specs=pl.BlockSpec((blk, width), lambda b, i, jj: (b * nb + i, 0)),
        out_shape=jax.ShapeDtypeStruct((batch * seq, width), BF16),
        scratch_shapes=[pltpu.VMEM((blk, width), F32),
                        pltpu.VMEM((n_heads, blk, 1), F32)],
        compiler_params=_cparams(("parallel", "parallel", "arbitrary")),
        name="sb_prompt")(qarr, kvarr, kvarr)


def _block_diag_queries(q, n_heads, t_new):
    col_head = lax.broadcasted_iota(jnp.int32, q.shape, 1) // HEAD_DIM
    rows = [jnp.where(col_head == h, q, 0.0) for h in range(n_heads)]
    return jnp.concatenate(rows, axis=0).astype(BF16)


def _sample_attn_body(pt_ref, qa_ref, qc_ref, nka_ref, nva_ref, nkc_ref, nvc_ref, nlf_ref,
                      cka_ref, cva_ref, clf_ref, ckc_ref, cvc_ref,
                      oa_ref, oc_ref,
                      qbda_ref, qbdc_ref, acca_ref, accc_ref, m_ref, l_ref, cara_ref, carc_ref,
                      *, n_heads, t_new, past_len, page, scale):
    del pt_ref
    step = pl.program_id(1)
    n_steps = pl.num_programs(1)
    rows = n_heads * t_new

    @pl.when(step == 0)
    def _():
        qbda_ref[...] = _block_diag_queries(qa_ref[...], n_heads, t_new)
        qbdc_ref[...] = _block_diag_queries(qc_ref[...], n_heads, t_new)
        acca_ref[...] = jnp.zeros_like(acca_ref)
        accc_ref[...] = jnp.zeros_like(accc_ref)
        m_ref[...] = jnp.full_like(m_ref, NEG_BIG)
        l_ref[...] = jnp.zeros_like(l_ref)
        cara_ref[...] = jnp.zeros_like(cara_ref)
        carc_ref[...] = jnp.zeros_like(carc_ref)

    def process(ka, va, lft, kc, vc, kbase):
        t_row = lax.broadcasted_iota(jnp.int32, (rows, page), 0) % t_new
        kpos = kbase + lax.broadcasted_iota(jnp.int32, (rows, page), 1)
        qpos = past_len + t_row
        valid_le = kpos <= qpos
        valid_lt = kpos < qpos
        r = lax.broadcasted_iota(jnp.int32, (page, page), 0)
        c = lax.broadcasted_iota(jnp.int32, (page, page), 1)
        later = jnp.where(r > c, 1.0, 0.0).astype(BF16)

        s = lax.dot_general(qbda_ref[...], ka, (((1,), (1,)), ((), ())),
                            preferred_element_type=F32) * scale
        lfe = jnp.concatenate(
            [jnp.broadcast_to(lft[h:h + 1, :], (t_new, page)) for h in range(n_heads)], axis=0)
        lfm = jnp.where(valid_le, lfe, 0.0)
        decay = _split_dot(lfm, later, 3) + cara_ref[...]
        s = jnp.where(valid_le, s + decay, NEG_BIG)
        m_prev = m_ref[...]
        m_new = jnp.maximum(m_prev, jnp.max(s, axis=-1, keepdims=True))
        p = jnp.exp(s - m_new)
        a = jnp.exp(m_prev - m_new)
        l_ref[...] = a * l_ref[...] + jnp.sum(p, axis=-1, keepdims=True)
        acca_ref[...] = a * acca_ref[...] + jnp.dot(p.astype(BF16), va, preferred_element_type=F32)
        m_ref[...] = m_new
        cara_ref[...] = cara_ref[...] + jnp.sum(lfm, axis=-1, keepdims=True)

        z = lax.dot_general(qbdc_ref[...], kc, (((1,), (1,)), ((), ())),
                            preferred_element_type=F32) * scale
        log_1mb = _log_sigmoid(-z)
        lm = jnp.where(valid_lt, log_1mb, 0.0)
        after = _split_dot(lm, later, 2) + carc_ref[...]
        w = jnp.where(valid_lt, jnp.exp(z + log_1mb + after), 0.0)
        carc_ref[...] = carc_ref[...] + jnp.sum(lm, axis=-1, keepdims=True)
        accc_ref[...] = accc_ref[...] + jnp.dot(w.astype(BF16), vc, preferred_element_type=F32)

    def pad_rows(x):
        return jnp.concatenate([x, jnp.zeros((page - t_new, x.shape[1]), x.dtype)], axis=0)

    @pl.when(step == 0)
    def _():
        process(pad_rows(nka_ref[...]).astype(BF16), pad_rows(nva_ref[...]).astype(BF16),
                nlf_ref[...],
                pad_rows(nkc_ref[...]).astype(BF16), pad_rows(nvc_ref[...]).astype(BF16),
                past_len)

    @pl.when(step > 0)
    def _():
        process(cka_ref[...].astype(BF16), cva_ref[...].astype(BF16), clf_ref[...],
                ckc_ref[...].astype(BF16), cvc_ref[...].astype(BF16),
                (n_steps - 1 - step) * page)

    @pl.when(step == n_steps - 1)
    def _():
        inv_l = 1.0 / l_ref[...]
        for h in range(n_heads):
            hs = slice(h * HEAD_DIM, (h + 1) * HEAD_DIM)
            rs = slice(h * t_new, (h + 1) * t_new)
            oa_ref[:, hs] = acca_ref[rs, hs] * inv_l[rs, :]
            oc_ref[:, hs] = accc_ref[rs, hs]


def _sample_attention(page_table, qa, qc, f32out, row_blk0, nlf_t, cache_k_a, cache_v_a,
                      cache_lf_t, cache_k_c, cache_v_c, layer, t_new, width):
    dec_batch, n_pages = page_table.shape
    page = cache_k_a.shape[2]
    n_heads = width // HEAD_DIM
    rows = n_heads * t_new
    past_len = n_pages * page
    n_steps = n_pages + 1

    def new_spec(col):
        return pl.BlockSpec((t_new, width), lambda b, s, pt: (row_blk0 + b, col))

    def page_of(b, s, pt):
        return pt[b * n_pages + jnp.maximum(n_pages - s, 0) - jnp.where(s == 0, 1, 0)]

    def cache_spec(last):
        return pl.BlockSpec((None, None) + last, lambda b, s, pt: (layer, page_of(b, s, pt), 0, 0))

    q_spec = pl.BlockSpec((t_new, width), lambda b, s, pt: (b, 0))
    kernel = functools.partial(_sample_attn_body, n_heads=n_heads, t_new=t_new,
                               past_len=past_len, page=page, scale=HEAD_DIM ** -0.5)
    out = jax.ShapeDtypeStruct((dec_batch * t_new, width), F32)
    return pl.pallas_call(
        kernel,
        grid_spec=pltpu.PrefetchScalarGridSpec(
            num_scalar_prefetch=1, grid=(dec_batch, n_steps),
            in_specs=[q_spec, q_spec, new_spec(0), new_spec(1), new_spec(2), new_spec(3),
                      pl.BlockSpec((None, n_heads, page), lambda b, s, pt: (b, 0, 0)),
                      cache_spec((page, width)), cache_spec((page, width)),
                      cache_spec((n_heads, page)),
                      cache_spec((page, width)), cache_spec((page, width))],
            out_specs=[q_spec, q_spec],
            scratch_shapes=[pltpu.VMEM((rows, width), BF16), pltpu.VMEM((rows, width), BF16),
                            pltpu.VMEM((rows, width), F32), pltpu.VMEM((rows, width), F32),
                            pltpu.VMEM((rows, 1), F32), pltpu.VMEM((rows, 1), F32),
                            pltpu.VMEM((rows, 1), F32), pltpu.VMEM((rows, 1), F32)]),
        out_shape=(out, out),
        compiler_params=_cparams(("parallel", "arbitrary")), name="sample_attn")(
            page_table.reshape(-1), qa, qc, f32out, f32out, f32out, f32out, nlf_t,
            cache_k_a, cache_v_a, cache_lf_t, cache_k_c, cache_v_c)


def _rglru_body(u_ref, g_ref, cw_ref, cb_ref, wa_ref, ba_ref, wx_ref, bx_ref, lam_ref,
                h0_ref, buf_ref, y_ref, hl_ref, nb_ref, xx_ref, h_ref):
    ti = pl.program_id(2)
    nt = pl.num_programs(2)
    tt, cw = u_ref.shape
    hist = SUBLANES
    taps = CONV_WIDTH - 1

    @pl.when(ti == 0)
    def _():
        xx_ref[hist - taps:hist, :] = buf_ref[...]
        h_ref[...] = h0_ref[...]

    xx_ref[hist:hist + tt, :] = u_ref[...]
    xc = cb_ref[...] + sum(
        xx_ref[hist - taps + k:hist - taps + k + tt, :] * cw_ref[k:k + 1, :]
        for k in range(CONV_WIDTH))

    rs, is_ = [], []
    for n in range(cw // LANES):
        cs = slice(n * LANES, (n + 1) * LANES)
        xb = xc[:, cs].astype(BF16)
        rs.append(jnp.dot(xb, wa_ref[n].astype(BF16), preferred_element_type=F32))
        is_.append(jnp.dot(xb, wx_ref[n].astype(BF16), preferred_element_type=F32))
    r = jax.nn.sigmoid(jnp.concatenate(rs, axis=1) + ba_ref[...])
    i = jax.nn.sigmoid(jnp.concatenate(is_, axis=1) + bx_ref[...])
    lam = lam_ref[...]
    softplus_neg_lam = jnp.maximum(-lam, 0.0) + jnp.log1p(jnp.exp(-jnp.abs(lam)))
    log_a = -RG_C * r * softplus_neg_lam
    a = jnp.exp(log_a)
    th = jnp.tanh(log_a)
    bt = jnp.sqrt(-2.0 * th / (1.0 - th)) * (i * xc)

    row = lax.broadcasted_iota(jnp.int32, (tt, cw), 0)
    d = 1
    while d < tt:
        keep = row >= d
        a_sh = jnp.where(keep, pltpu.roll(a, d, 0), 1.0)
        b_sh = jnp.where(keep, pltpu.roll(bt, d, 0), 0.0)
        bt = a * b_sh + bt
        a = a * a_sh
        d *= 2
    h = a * h_ref[...] + bt
    y_ref[...] = (h * jax.nn.gelu(g_ref[...].astype(F32))).astype(y_ref.dtype)
    h_ref[...] = h[tt - 1:tt, :]
    xx_ref[0:hist, :] = xx_ref[tt:tt + hist, :]

    @pl.when(ti == nt - 1)
    def _():
        hl_ref[...] = h[tt - 1:tt, :]
        nb_ref[...] = xx_ref[hist - taps:hist, :]


def _rglru(u_arr, u_col0, g_arr, g_col0, row_blk0, batch, t_len, conv_w, conv_b, w_rg_a, b_rg_a,
           w_rg_x, b_rg_x, lam, h0, buf, layer, y_dtype):
    width = lam.shape[-1]
    cw = _tile(width, RG_CW, LANES)
    tt = _tile(t_len, RG_TT, SUBLANES)
    nt = t_len // tt
    nc = width // cw
    nblk = cw // LANES
    assert t_len >= CONV_WIDTH - 1 and tt >= SUBLANES

    def tok(col0):
        return pl.BlockSpec((tt, cw), lambda b, c, t: (row_blk0 + b * nt + t, col0 // cw + c))

    vec = pl.BlockSpec((None, 1, cw), lambda b, c, t: (layer, 0, c))
    gate_w = pl.BlockSpec((None, nblk, LANES, LANES), lambda b, c, t: (layer, c, 0, 0))
    state = pl.BlockSpec((None, 1, cw), lambda b, c, t: (b, 0, c))
    hist = pl.BlockSpec((None, CONV_WIDTH - 1, cw), lambda b, c, t: (b, 0, c))
    return pl.pallas_call(
        _rglru_body, grid=(batch, nc, nt),
        in_specs=[tok(u_col0), tok(g_col0),
                  pl.BlockSpec((None, CONV_WIDTH, cw), lambda b, c, t: (layer, 0, c)),
                  vec, gate_w, vec, gate_w, vec, vec, state, hist],
        out_specs=[pl.BlockSpec((tt, cw), lambda b, c, t: (b * nt + t, c)), state, hist],
        out_shape=(jax.ShapeDtypeStruct((batch * t_len, width), y_dtype),
                   jax.ShapeDtypeStruct((batch, 1, width), F32),
                   jax.ShapeDtypeStruct((batch, CONV_WIDTH - 1, width), F32)),
        scratch_shapes=[pltpu.VMEM((tt + 2 * SUBLANES, cw), F32), pltpu.VMEM((1, cw), F32)],
        compiler_params=_cparams(("parallel", "parallel", "arbitrary")), name="rglru")(
            u_arr, g_arr, conv_w, conv_b, w_rg_a, b_rg_a, w_rg_x, b_rg_x, lam, h0, buf)


INFO_E1, INFO_E2, INFO_R1, INFO_R2, INFO_W1, INFO_W2 = range(6)


def _router_body(x_ref, w_ref, b_ref, info_ref, cnt_ref, run_ref, *, n_experts):
    tm = x_ref.shape[0]
    epg = n_experts // N_GROUPS

    @pl.when(pl.program_id(0) == 0)
    def _():
        run_ref[...] = jnp.zeros_like(run_ref)

    logits = jnp.dot(x_ref[...], w_ref[...], preferred_element_type=F32,
                     precision=lax.Precision.HIGHEST) + b_ref[...]
    lane_i = lax.broadcasted_iota(jnp.int32, (tm, LANES), 1)
    lane = lane_i.astype(F32)
    mx = jnp.max(logits, axis=-1, keepdims=True)
    e = jnp.exp(logits - mx)
    probs = e / jnp.sum(e, axis=-1, keepdims=True)

    def top1(vals):
        top = jnp.max(vals, axis=-1, keepdims=True)
        idx = jnp.min(jnp.where(vals == top, lane, float(LANES)), axis=-1, keepdims=True)
        return top, idx

    best = None
    for g in range(N_GROUPS):
        in_g = (lane_i >= g * epg) & (lane_i < (g + 1) * epg)
        pg = jnp.where(in_g, probs, -1.0)
        v1, i1 = top1(pg)
        v2, i2 = top1(jnp.where(lane == i1, -1.0, pg))
        score = v1 + v2
        if best is None:
            best = (score, v1, i1, v2, i2)
        else:
            better = score > best[0]
            best = tuple(jnp.where(better, new, old)
                         for new, old in zip((score, v1, i1, v2, i2), best))
    _, v1, i1, v2, i2 = best
    denom = v1 + v2
    w1 = v1 / denom
    w2 = v2 / denom

    hit1 = lane == i1
    hit2 = lane == i2
    cnt = jnp.where(hit1 | hit2, 1.0, 0.0)
    r = lax.broadcasted_iota(jnp.int32, (tm, tm), 0)
    c = lax.broadcasted_iota(jnp.int32, (tm, tm), 1)
    before = jnp.where(c < r, 1.0, 0.0).astype(BF16)
    prefix = jnp.dot(before, cnt.astype(BF16), preferred_element_type=F32) + run_ref[...]
    r1 = jnp.sum(jnp.where(hit1, prefix, 0.0), axis=-1, keepdims=True)
    r2 = jnp.sum(jnp.where(hit2, prefix, 0.0), axis=-1, keepdims=True)
    run_ref[...] = run_ref[...] + jnp.sum(cnt, axis=0, keepdims=True)

    info = jnp.zeros((tm, LANES), F32)
    for col, val in ((INFO_E1, i1), (INFO_E2, i2), (INFO_R1, r1), (INFO_R2, r2),
                     (INFO_W1, w1), (INFO_W2, w2)):
        info = jnp.where(lane_i == col, val, info)
    info_ref[...] = info
    cnt_ref[...] = run_ref[...]


def _router(x, w_pad, b_pad, n_experts):
    m, d = x.shape
    tm = _tile(m, TOK_TM, ROW_ALIGN)
    return pl.pallas_call(
        functools.partial(_router_body, n_experts=n_experts), grid=(m // tm,),
        in_specs=[pl.BlockSpec((tm, d), lambda i: (i, 0)),
                  pl.BlockSpec((d, LANES), lambda i: (0, 0)),
                  pl.BlockSpec((1, LANES), lambda i: (0, 0))],
        out_specs=[pl.BlockSpec((tm, LANES), lambda i: (i, 0)),
                   pl.BlockSpec((1, LANES), lambda i: (0, 0))],
        out_shape=(jax.ShapeDtypeStruct((m, LANES), F32), jax.ShapeDtypeStruct((1, LANES), F32)),
        scratch_shapes=[pltpu.VMEM((1, LANES), F32)],
        compiler_params=_cparams(("arbitrary",)), name="router")(x, w_pad, b_pad)


def _dispatch_body(pos_ref, x_hbm, xs_in_hbm, xs_hbm, sem, *, tm):
    del xs_in_hbm
    base = pl.program_id(0) * tm

    def row_copy(n, slot):
        return pltpu.make_async_copy(x_hbm.at[pl.ds(n, 1)],
                                     xs_hbm.at[pl.ds(pos_ref[TOP_K * n + slot], 1)], sem)

    def start(n, carry):
        for slot in range(TOP_K):
            row_copy(base + n, slot).start()
        return carry

    def wait(n, carry):
        for slot in range(TOP_K):
            row_copy(base + n, slot).wait()
        return carry

    lax.fori_loop(0, tm, start, 0)
    lax.fori_loop(0, tm, wait, 0)


def _dispatch(x, pos, n_rows):
    m, d = x.shape
    tm = _tile(m, TOK_TM, ROW_ALIGN)
    xs0 = jnp.zeros((n_rows, d), x.dtype)
    return pl.pallas_call(
        functools.partial(_dispatch_body, tm=tm),
        grid_spec=pltpu.PrefetchScalarGridSpec(
            num_scalar_prefetch=1, grid=(m // tm,),
            in_specs=[pl.BlockSpec(memory_space=pl.ANY), pl.BlockSpec(memory_space=pl.ANY)],
            out_specs=pl.BlockSpec(memory_space=pl.ANY),
            scratch_shapes=[pltpu.SemaphoreType.DMA(())]),
        out_shape=jax.ShapeDtypeStruct((n_rows, d), x.dtype),
        input_output_aliases={2: 0},
        compiler_params=_cparams(("arbitrary",)), name="moe_dispatch")(pos, x, xs0)


def _ffn1_body(te_ref, nv_ref, x_ref, wg_ref, wu_ref, h_ref):
    del te_ref

    @pl.when(pl.program_id(0) < nv_ref[0])
    def _():
        xb = x_ref[...].astype(BF16)
        g = jnp.dot(xb, wg_ref[...].astype(BF16), preferred_element_type=F32)
        u = jnp.dot(xb, wu_ref[...].astype(BF16), preferred_element_type=F32)
        h_ref[...] = (g * jax.nn.sigmoid(g) * u).astype(h_ref.dtype)

    @pl.when(pl.program_id(0) >= nv_ref[0])
    def _():
        h_ref[...] = jnp.zeros_like(h_ref)


def _ffn2_body(te_ref, nv_ref, h_ref, wd_ref, y_ref):
    del te_ref

    @pl.when(pl.program_id(0) < nv_ref[0])
    def _():
        y_ref[...] = jnp.dot(h_ref[...], wd_ref[...].astype(BF16), preferred_element_type=F32)

    @pl.when(pl.program_id(0) >= nv_ref[0])
    def _():
        y_ref[...] = jnp.zeros_like(y_ref)


def _expert_ffn(xs, tile_expert, n_valid, w_gate, w_up, w_down, layer):
    n_rows, d = xs.shape
    f = w_gate.shape[-1]
    n_tiles = n_rows // MOE_TM
    row = lambda width: pl.BlockSpec((MOE_TM, width), lambda t, te, nv: (t, 0))
    wspec = lambda a, b: pl.BlockSpec((None, None, a, b), lambda t, te, nv: (layer, te[t], 0, 0))
    hid = pl.pallas_call(
        _ffn1_body,
        grid_spec=pltpu.PrefetchScalarGridSpec(
            num_scalar_prefetch=2, grid=(n_tiles,),
            in_specs=[row(d), wspec(d, f), wspec(d, f)], out_specs=row(f)),
        out_shape=jax.ShapeDtypeStruct((n_rows, f), BF16),
        compiler_params=_cparams(("arbitrary",)), name="moe_ffn1")(
            tile_expert, n_valid, xs, w_gate, w_up)
    return pl.pallas_call(
        _ffn2_body,
        grid_spec=pltpu.PrefetchScalarGridSpec(
            num_scalar_prefetch=2, grid=(n_tiles,),
            in_specs=[row(f), wspec(f, d)], out_specs=row(d)),
        out_shape=jax.ShapeDtypeStruct((n_rows, d), F32),
        compiler_params=_cparams(("arbitrary",)), name="moe_ffn2")(
            tile_expert, n_valid, hid, w_down)


def _combine_body(pos_ref, x_ref, info_ref, g_ref, b_ref, y_hbm, o_ref, obf_ref, ybuf, sem,
                  *, tm, alpha):
    base = pl.program_id(0) * tm

    def row_copy(n, slot):
        return pltpu.make_async_copy(y_hbm.at[pl.ds(pos_ref[TOP_K * (base + n) + slot], 1)],
                                     ybuf.at[slot, pl.ds(n, 1)], sem)

    def start(n, carry):
        for slot in range(TOP_K):
            row_copy(n, slot).start()
        return carry

    def wait(n, carry):
        for slot in range(TOP_K):
            row_copy(n, slot).wait()
        return carry

    lax.fori_loop(0, tm, start, 0)
    lax.fori_loop(0, tm, wait, 0)
    info = info_ref[...]
    moe = info[:, INFO_W1:INFO_W1 + 1] * ybuf[0] + info[:, INFO_W2:INFO_W2 + 1] * ybuf[1]
    out = _layernorm_rows(alpha * x_ref[...] + moe, g_ref[...], b_ref[...])
    o_ref[...] = out
    obf_ref[...] = out.astype(BF16)


def _combine_ln(x, info, pos, y, g, b, layer, alpha):
    m, d = x.shape
    tm = _tile(m, TOK_TM, ROW_ALIGN)
    row = pl.BlockSpec((tm, d), lambda i, pos: (i, 0))
    vec = pl.BlockSpec((None, 1, d), lambda i, pos: (layer, 0, 0))
    return pl.pallas_call(
        functools.partial(_combine_body, tm=tm, alpha=alpha),
        grid_spec=pltpu.PrefetchScalarGridSpec(
            num_scalar_prefetch=1, grid=(m // tm,),
            in_specs=[row, pl.BlockSpec((tm, LANES), lambda i, pos: (i, 0)), vec, vec,
                      pl.BlockSpec(memory_space=pl.ANY)],
            out_specs=[row, row],
            scratch_shapes=[pltpu.VMEM((TOP_K, tm, d), F32), pltpu.SemaphoreType.DMA(())]),
        out_shape=(jax.ShapeDtypeStruct((m, d), F32), jax.ShapeDtypeStruct((m, d), BF16)),
        compiler_params=_cparams(("arbitrary",)), name="moe_combine_ln2")(pos, x, info, g, b, y)


def _moe_plan(info, counts, n_experts, n_tiles):
    e = info[:, INFO_E1:INFO_E2 + 1].astype(jnp.int32)
    rank = info[:, INFO_R1:INFO_R2 + 1].astype(jnp.int32)
    cnt = counts[0, :n_experts].astype(jnp.int32)
    tiles_per = (cnt + MOE_TM - 1) // MOE_TM
    tile_end = jnp.cumsum(tiles_per)
    tile_start = tile_end - tiles_per
    pos = (tile_start * MOE_TM)[e] + rank
    n_valid = tile_end[-1]
    t = jnp.minimum(jnp.arange(n_tiles, dtype=jnp.int32), n_valid - 1)
    tile_expert = jnp.sum(t[:, None] >= tile_end[None, :], axis=1).astype(jnp.int32)
    return pos.reshape(-1), tile_expert, n_valid.reshape(1).astype(jnp.int32)


def kernel(x_prompt, x_sample, cache_k_a, cache_v_a, cache_lf_a, cache_k_c, cache_v_c, state_h_b,
           state_conv_b, page_table, w_in, b_f, conv_w, conv_b, w_rg_a, b_rg_a, w_rg_x, b_rg_x, lam,
           w_proj_a, w_proj_b, w_proj_c, w_out, ln1_g, ln1_b, w_router, b_router, w_e_gate, w_e_up,
           w_e_down, ln2_g, ln2_b):
    batch, seq, d = x_prompt.shape
    dec_batch, t_new, _ = x_sample.shape
    depth = w_in.shape[0]
    d_a = w_proj_a.shape[1]
    w_b = w_proj_b.shape[1]
    d_c = w_proj_c.shape[1]
    h_a = b_f.shape[1]
    n_experts = w_router.shape[1]
    alpha = (2.0 * depth) ** 0.25
    n_p = batch * seq
    n_s = dec_batch * t_new
    m_all = -(-(n_p + n_s) // ROW_ALIGN) * ROW_ALIGN
    assert d_a == d_c and d_a % MM_TN == 0 and n_p % SUBLANES == 0 and h_a <= SUBLANES

    o_q, o_k, o_v, o_f = 0, d_a, 2 * d_a, 3 * d_a
    o_u = o_f + h_a
    o_g = o_u + w_b
    o_qc = o_g + w_b
    o_kc, o_vc = o_qc + d_c, o_qc + 2 * d_c
    o_gl = o_qc + 3 * d_c
    seg = lambda a, n: w_in[:, :, a:a + n]
    w_cat = jnp.concatenate(
        [seg(o_k, d_a), seg(o_v, d_a), seg(o_kc, d_c), seg(o_vc, d_c), seg(o_u, w_b),
         seg(o_q, d_a), seg(o_qc, d_c), seg(o_g, w_b),
         seg(o_gl, 3 * d),
         seg(o_f, h_a), jnp.zeros((depth, d, LANES - h_a), w_in.dtype)], axis=2).astype(BF16)
    n_kv = 2 * d_a + 2 * d_c + w_b
    n_qg = d_a + d_c + w_b
    blk_qg = n_kv // MM_TN
    blk_gl = (n_kv + n_qg) // MM_TN
    blk_f = (n_kv + n_qg + 3 * d) // LANES
    bias_f = jnp.concatenate([b_f, jnp.zeros((depth, LANES - h_a), b_f.dtype)], axis=1)[:, None, :]

    vec3 = lambda a: a.reshape(depth, 1, a.shape[-1])
    conv_b3, lam3 = vec3(conv_b), vec3(lam)
    b_rg_a3 = b_rg_a.reshape(depth, 1, w_b)
    b_rg_x3 = b_rg_x.reshape(depth, 1, w_b)
    ln1_g3, ln1_b3, ln2_g3, ln2_b3 = vec3(ln1_g), vec3(ln1_b), vec3(ln2_g), vec3(ln2_b)
    w_r_pad = jnp.concatenate([w_router, jnp.zeros((d, LANES - n_experts), w_router.dtype)], axis=1)
    b_r_pad = jnp.concatenate([b_router, jnp.full((LANES - n_experts,), NEG_BIG, b_router.dtype)])[None, :]

    n_pool, page = cache_k_a.shape[1], cache_k_a.shape[2]
    ck_a = cache_k_a.reshape(depth, n_pool, page, d_a)
    cv_a = cache_v_a.reshape(depth, n_pool, page, d_a)
    ck_c = cache_k_c.reshape(depth, n_pool, page, d_c)
    cv_c = cache_v_c.reshape(depth, n_pool, page, d_c)
    clf_t = jnp.swapaxes(cache_lf_a, 2, 3)

    zeros_h = jnp.zeros((batch, 1, w_b), F32)
    zeros_buf = jnp.zeros((batch, CONV_WIDTH - 1, w_b), F32)
    n_tiles = (TOP_K * m_all + n_experts * (MOE_TM - 1) + MOE_TM - 1) // MOE_TM

    x = jnp.concatenate([x_prompt.reshape(n_p, d), x_sample.reshape(n_s, d),
                         jnp.zeros((m_all - n_p - n_s, d), F32)], axis=0)
    x_bf = x.astype(BF16)
    pad_rows = lambda a: jnp.concatenate(
        [a, jnp.zeros((m_all - a.shape[0], a.shape[1]), a.dtype)], axis=0)

    new_p = [[] for _ in range(7)]
    new_s = [[] for _ in range(7)]
    for l in range(depth):
        kv32, kv16 = _matmul(x_bf, w_cat, l, 0, n_kv, (F32, BF16), tn=MM_TN, name="inproj_kv")
        (qg16,) = _matmul(x_bf, w_cat, l, blk_qg, n_qg, (BF16,), tn=MM_TN, name="inproj_qg")
        (gates,) = _matmul(x_bf, w_cat, l, blk_gl, 3 * d, (BF16,), tn=MM_TN, epilogue="sigmoid",
                           name="inproj_gates")
        (lf,) = _matmul(x_bf, w_cat, l, blk_f, LANES, (F32,), tn=LANES, epilogue="logsig_bias",
                        extra=bias_f, name="inproj_forget")

        c_col = _prompt_cumsum(lf, batch, seq)
        c_row = jnp.swapaxes(c_col.reshape(batch, seq, LANES)[:, :, :SUBLANES], 1, 2)
        ya_p = _fox_prompt(qg16, kv16, c_col, c_row, batch, seq, d_a, 0, 0, 1)
        yc_p = _sb_prompt(qg16, kv16, batch, seq, d_c, 1, 2, 3)
        yb_p, hl_p, nb_p = _rglru(kv32, 2 * d_a + 2 * d_c, qg16, d_a + d_c, 0, batch, seq,
                                  conv_w, conv_b3, w_rg_a, b_rg_a3, w_rg_x, b_rg_x3, lam3,
                                  zeros_h, zeros_buf, l, BF16)

        qg_s = qg16[n_p:n_p + n_s].astype(F32)
        lf_s = lf[n_p:n_p + n_s, :SUBLANES].reshape(dec_batch, t_new, SUBLANES)
        nlf_t = jnp.concatenate([jnp.swapaxes(lf_s, 1, 2),
                                 jnp.zeros((dec_batch, SUBLANES, page - t_new), F32)], axis=2)
        ya_s, yc_s = _sample_attention(page_table, qg_s[:, :d_a], qg_s[:, d_a:d_a + d_c], kv32,
                                       n_p // t_new, nlf_t, ck_a, cv_a, clf_t, ck_c, cv_c, l,
                                       t_new, d_a)
        yb_s, hl_s, nb_s = _rglru(kv32[n_p:n_p + n_s, 2 * d_a + 2 * d_c:], 0,
                                  qg_s[:, d_a + d_c:], 0, 0, dec_batch, t_new,
                                  conv_w, conv_b3, w_rg_a, b_rg_a3, w_rg_x, b_rg_x3, lam3,
                                  state_h_b[l][:, None, :], state_conv_b[l], l, F32)

        ya = pad_rows(jnp.concatenate([ya_p, ya_s.astype(BF16)], axis=0))
        yb = pad_rows(jnp.concatenate([yb_p, yb_s.astype(BF16)], axis=0))
        yc = pad_rows(jnp.concatenate([yc_p, yc_s.astype(BF16)], axis=0))
        merged = _merge(ya, yb, yc, w_proj_a, w_proj_b, w_proj_c, gates, l)
        (pre1,) = _matmul(merged, w_out, l, 0, d, (F32,), tn=MM_TN, epilogue="residual", extra=x,
                          alpha=alpha, name="out_proj")
        x1 = _layernorm(pre1, ln1_g3, ln1_b3, l)

        info, counts = _router(x1, w_r_pad, b_r_pad, n_experts)
        pos, tile_expert, n_valid = _moe_plan(info, counts, n_experts, n_tiles)
        xs = _dispatch(x1, pos, n_tiles * MOE_TM)
        y = _expert_ffn(xs, tile_expert, n_valid, w_e_gate, w_e_up, w_e_down, l)
        x, x_bf = _combine_ln(x1, info, pos, y, ln2_g3, ln2_b3, l, alpha)

        def heads(a, rows0, nrows, col0, lead):
            return a[rows0:rows0 + nrows, col0:col0 + d_a].reshape(lead + (d_a // HEAD_DIM, HEAD_DIM))

        for dst, rows0, nrows, lead in ((new_p, 0, n_p, (batch, seq)),
                                        (new_s, n_p, n_s, (dec_batch, t_new))):
            dst[0].append(heads(kv32, rows0, nrows, 0, lead))
            dst[1].append(heads(kv32, rows0, nrows, d_a, lead))
            dst[2].append(lf[rows0:rows0 + nrows, :h_a].reshape(lead + (h_a,)))
            dst[3].append(heads(kv32, rows0, nrows, 2 * d_a, lead))
            dst[4].append(heads(kv32, rows0, nrows, 2 * d_a + d_c, lead))
        new_p[5].append(hl_p[:, 0, :])
        new_p[6].append(nb_p)
        new_s[5].append(hl_s[:, 0, :])
        new_s[6].append(nb_s)

    y_prompt = x[:n_p].reshape(batch, seq, d)
    y_sample = x[n_p:n_p + n_s].reshape(dec_batch, t_new, d)
    return (y_prompt, y_sample) + tuple(jnp.stack(s) for s in new_p) + tuple(jnp.stack(s) for s in new_s)
```

```python
import functools

import jax
import jax.numpy as jnp
from jax import lax
from jax.experimental import pallas as pl
from jax.experimental.pallas import tpu as pltpu

F32 = jnp.float32
BF16 = jnp.bfloat16

LANES = 128
SUBLANES = 8
VMEM_LIMIT_BYTES = 56 * 1024 * 1024

HEAD_DIM = 128
CONV_WIDTH = 4
RG_C = 8.0
N_GROUPS = 4
TOP_K = 2
LN_EPS = 1e-5
NEG_BIG = -1e30

ROW_ALIGN = 256
MM_TM = 768
MM_TN = 512
ATT_BLK = 256
RG_TT = 256
RG_CW = 512
MOE_TM = 256
TOK_TM = 256


def _cparams(sem):
    return pltpu.CompilerParams(dimension_semantics=sem, vmem_limit_bytes=VMEM_LIMIT_BYTES)


def _tile(n, pref, align):
    if n <= pref:
        return n
    t = (pref // align) * align
    while t >= align:
        if n % t == 0:
            return t
        t -= align
    raise ValueError(f"no tile for {n} (pref {pref}, align {align})")


def _softplus(x):
    return jnp.maximum(x, 0.0) + jnp.log(1.0 + jnp.exp(-jnp.abs(x)))


def _log_sigmoid(x):
    return jnp.minimum(x, 0.0) - jnp.log1p(jnp.exp(-jnp.abs(x)))


def _split_dot(a, t_bf16, parts):
    acc = None
    r = a
    for p in range(parts):
        piece = r.astype(BF16)
        d = jnp.dot(piece, t_bf16, preferred_element_type=F32)
        acc = d if acc is None else acc + d
        if p + 1 < parts:
            r = r - piece.astype(F32)
    return acc


def _later_matrix(n):
    r = lax.broadcasted_iota(jnp.int32, (n, n), 0)
    c = lax.broadcasted_iota(jnp.int32, (n, n), 1)
    return jnp.where(r > c, 1.0, 0.0).astype(BF16)


def _mm_body(*refs, epilogue, alpha):
    x_ref, w_ref = refs[0], refs[1]
    rest = refs[2:]
    acc = jnp.dot(x_ref[...], w_ref[...].astype(BF16), preferred_element_type=F32)
    if epilogue == "scale":
        acc = alpha * acc
    elif epilogue == "sigmoid":
        acc = jax.nn.sigmoid(acc)
    elif epilogue == "logsig_bias":
        acc = _log_sigmoid(acc + rest[0][...])
        rest = rest[1:]
    elif epilogue == "residual":
        acc = alpha * rest[0][...] + acc
        rest = rest[1:]
    for o_ref in rest:
        o_ref[...] = acc.astype(o_ref.dtype)


def _matmul(x, w, layer, col0, n_cols, out_dtypes, *, tn=MM_TN, epilogue="none", extra=None,
            alpha=1.0, name):
    m, k = x.shape
    tm = _tile(m, MM_TM, ROW_ALIGN)
    assert col0 % tn == 0 and n_cols % tn == 0
    col_blk0 = col0 // tn
    grid = (m // tm, n_cols // tn)
    in_specs = [
        pl.BlockSpec((tm, k), lambda i, j: (i, 0)),
        pl.BlockSpec((None, k, tn), lambda i, j: (layer, 0, j + col_blk0)),
    ]
    args = [x, w]
    if epilogue == "logsig_bias":
        in_specs.append(pl.BlockSpec((None, 1, tn), lambda i, j: (layer, 0, j)))
        args.append(extra)
    elif epilogue == "residual":
        in_specs.append(pl.BlockSpec((tm, tn), lambda i, j: (i, j)))
        args.append(extra)
    outs = tuple(jax.ShapeDtypeStruct((m, n_cols), dt) for dt in out_dtypes)
    out_specs = tuple(pl.BlockSpec((tm, tn), lambda i, j: (i, j)) for _ in out_dtypes)
    return pl.pallas_call(
        functools.partial(_mm_body, epilogue=epilogue, alpha=alpha),
        grid=grid, in_specs=in_specs, out_specs=out_specs, out_shape=outs,
        compiler_params=_cparams(("parallel", "arbitrary")), name=name)(*args)


def _merge_body(ya, yb, yc, wa, wb, wc, g0, g1, g2, o_ref):
    pa = jnp.dot(ya[...], wa[...].astype(BF16), preferred_element_type=F32)
    pb = jnp.dot(yb[...], wb[...].astype(BF16), preferred_element_type=F32)
    pc = jnp.dot(yc[...], wc[...].astype(BF16), preferred_element_type=F32)
    m = g0[...].astype(F32) * pa + g1[...].astype(F32) * pb + g2[...].astype(F32) * pc
    o_ref[...] = m.astype(o_ref.dtype)


def _merge(ya, yb, yc, w_proj_a, w_proj_b, w_proj_c, gates, layer):
    m = ya.shape[0]
    d = w_proj_a.shape[-1]
    tm = _tile(m, MM_TM, ROW_ALIGN)
    tn = MM_TN
    nb = d // tn

    def act(width):
        return pl.BlockSpec((tm, width), lambda i, j: (i, 0))

    def wgt(width):
        return pl.BlockSpec((None, width, tn), lambda i, j: (layer, 0, j))

    def gate(branch):
        return pl.BlockSpec((tm, tn), lambda i, j: (i, j + branch * nb))

    return pl.pallas_call(
        _merge_body, grid=(m // tm, nb),
        in_specs=[act(ya.shape[1]), act(yb.shape[1]), act(yc.shape[1]),
                  wgt(ya.shape[1]), wgt(yb.shape[1]), wgt(yc.shape[1]),
                  gate(0), gate(1), gate(2)],
        out_specs=pl.BlockSpec((tm, tn), lambda i, j: (i, j)),
        out_shape=jax.ShapeDtypeStruct((m, d), BF16),
        compiler_params=_cparams(("parallel", "arbitrary")), name="merge")(
            ya, yb, yc, w_proj_a, w_proj_b, w_proj_c, gates, gates, gates)


def _layernorm_rows(x, g, b):
    mu = jnp.mean(x, axis=-1, keepdims=True)
    xc = x - mu
    var = jnp.mean(xc * xc, axis=-1, keepdims=True)
    return xc * lax.rsqrt(var + LN_EPS) * g + b


def _ln_body(x_ref, g_ref, b_ref, o_ref):
    o_ref[...] = _layernorm_rows(x_ref[...], g_ref[...], b_ref[...])


def _layernorm(x, g, b, layer):
    m, d = x.shape
    tm = _tile(m, TOK_TM, ROW_ALIGN)
    vec = pl.BlockSpec((None, 1, d), lambda i: (layer, 0, 0))
    row = pl.BlockSpec((tm, d), lambda i: (i, 0))
    return pl.pallas_call(
        _ln_body, grid=(m // tm,), in_specs=[row, vec, vec], out_specs=row,
        out_shape=jax.ShapeDtypeStruct((m, d), F32),
        compiler_params=_cparams(("parallel",)), name="ln1")(x, g, b)


def _cumsum_body(lf_ref, c_ref, carry_ref):
    ts = lf_ref.shape[0]

    @pl.when(pl.program_id(1) == 0)
    def _():
        carry_ref[...] = jnp.zeros_like(carry_ref)

    r = lax.broadcasted_iota(jnp.int32, (ts, ts), 0)
    c = lax.broadcasted_iota(jnp.int32, (ts, ts), 1)
    lower = jnp.where(c <= r, 1.0, 0.0).astype(BF16)
    acc = None
    rem = lf_ref[...]
    for p in range(3):
        piece = rem.astype(BF16)
        dd = jnp.dot(lower, piece, preferred_element_type=F32)
        acc = dd if acc is None else acc + dd
        if p < 2:
            rem = rem - piece.astype(F32)
    out = acc + carry_ref[...]
    c_ref[...] = out
    carry_ref[...] = out[ts - 1:ts, :]


def _prompt_cumsum(lf, batch, seq):
    ts = _tile(seq, ATT_BLK, SUBLANES)
    nt = seq // ts
    spec = pl.BlockSpec((ts, LANES), lambda b, t: (b * nt + t, 0))
    return pl.pallas_call(
        _cumsum_body, grid=(batch, nt), in_specs=[spec], out_specs=spec,
        out_shape=jax.ShapeDtypeStruct((batch * seq, LANES), F32),
        scratch_shapes=[pltpu.VMEM((1, LANES), F32)],
        compiler_params=_cparams(("parallel", "arbitrary")), name="lf_cumsum")(lf)


def _fox_prompt_body(q_ref, k_ref, v_ref, ccol_ref, crow_ref, o_ref, acc_ref, m_ref, l_ref,
                     *, n_heads):
    i = pl.program_id(1)
    j = pl.program_id(2)
    tq = q_ref.shape[0]
    tk = k_ref.shape[0]

    @pl.when(j == 0)
    def _():
        acc_ref[...] = jnp.zeros_like(acc_ref)
        m_ref[...] = jnp.full_like(m_ref, NEG_BIG)
        l_ref[...] = jnp.zeros_like(l_ref)

    def step(diagonal):
        if diagonal:
            mask = (lax.broadcasted_iota(jnp.int32, (tq, tk), 1)
                    <= lax.broadcasted_iota(jnp.int32, (tq, tk), 0))
        heads = range(n_heads)
        hsl = [slice(h * HEAD_DIM, (h + 1) * HEAD_DIM) for h in heads]
        ss = [lax.dot_general(q_ref[:, hsl[h]], k_ref[:, hsl[h]], (((1,), (1,)), ((), ())),
                              preferred_element_type=F32)
              - (crow_ref[h:h + 1, :] - ccol_ref[0:1, h:h + 1]) for h in heads]
        if diagonal:
            ss = [jnp.where(mask, s, NEG_BIG) for s in ss]
        m_prev = [m_ref[h] for h in heads]
        m_new = [jnp.maximum(m_prev[h], jnp.max(ss[h], axis=-1, keepdims=True)) for h in heads]
        ps = [jnp.exp(ss[h] - m_new[h]) for h in heads]
        al = [jnp.exp(m_prev[h] - m_new[h]) for h in heads]
        for h in heads:
            l_ref[h] = al[h] * l_ref[h] + jnp.sum(ps[h], axis=-1, keepdims=True)
            m_ref[h] = m_new[h]
        pv = [jnp.dot(ps[h].astype(BF16), v_ref[:, hsl[h]], preferred_element_type=F32)
              for h in heads]
        for h in heads:
            acc_ref[:, hsl[h]] = al[h] * acc_ref[:, hsl[h]] + pv[h]

    @pl.when(j < i)
    def _():
        step(False)

    @pl.when(j == i)
    def _():
        step(True)
        for h in range(n_heads):
            hs = slice(h * HEAD_DIM, (h + 1) * HEAD_DIM)
            o_ref[:, hs] = (acc_ref[:, hs] / l_ref[h]).astype(o_ref.dtype)


def _fox_prompt(q, kv, c_col, c_row, batch, seq, width):
    n_heads = width // HEAD_DIM
    blk = _tile(seq, ATT_BLK, LANES)
    nb = seq // blk
    return pl.pallas_call(
        functools.partial(_fox_prompt_body, n_heads=n_heads), grid=(batch, nb, nb),
        in_specs=[
            pl.BlockSpec((blk, width), lambda b, i, j: (b * nb + i, 0)),
            pl.BlockSpec((blk, width), lambda b, i, j: (b * nb + jnp.minimum(j, i), 0)),
            pl.BlockSpec((blk, width), lambda b, i, j: (b * nb + jnp.minimum(j, i), 1)),
            pl.BlockSpec((blk, LANES), lambda b, i, j: (b * nb + i, 0)),
            pl.BlockSpec((None, SUBLANES, blk), lambda b, i, j: (b, 0, jnp.minimum(j, i))),
        ],
        out_specs=pl.BlockSpec((blk, width), lambda b, i, j: (b * nb + i, 0)),
        out_shape=jax.ShapeDtypeStruct((batch * seq, width), BF16),
        scratch_shapes=[pltpu.VMEM((blk, width), F32),
                        pltpu.VMEM((n_heads, blk, 1), F32),
                        pltpu.VMEM((n_heads, blk, 1), F32)],
        compiler_params=_cparams(("parallel", "parallel", "arbitrary")),
        name="fox_prompt")(q, kv, kv, c_col, c_row)


def _sb_prompt_body(q_ref, k_ref, v_ref, o_ref, acc_ref, carry_ref, *, n_heads):
    i = pl.program_id(1)
    jj = pl.program_id(2)
    tq = q_ref.shape[0]
    tk = k_ref.shape[0]

    @pl.when(jj == 0)
    def _():
        acc_ref[...] = jnp.zeros_like(acc_ref)
        carry_ref[...] = jnp.zeros_like(carry_ref)

    def step(diagonal):
        later = _later_matrix(tk)
        if diagonal:
            mask = (lax.broadcasted_iota(jnp.int32, (tq, tk), 1)
                    < lax.broadcasted_iota(jnp.int32, (tq, tk), 0))
        heads = range(n_heads)
        hsl = [slice(h * HEAD_DIM, (h + 1) * HEAD_DIM) for h in heads]
        zs = [lax.dot_general(q_ref[:, hsl[h]], k_ref[:, hsl[h]], (((1,), (1,)), ((), ())),
                              preferred_element_type=F32) for h in heads]
        log_1mb = [-_softplus(z) for z in zs]
        lms = [jnp.where(mask, x, 0.0) for x in log_1mb] if diagonal else log_1mb
        after = [_split_dot(lms[h], later, 2) + carry_ref[h] for h in heads]
        ws = [jnp.exp(zs[h] + log_1mb[h] + after[h]) for h in heads]
        if diagonal:
            ws = [jnp.where(mask, w, 0.0) for w in ws]
        for h in heads:
            carry_ref[h] = carry_ref[h] + jnp.sum(lms[h], axis=-1, keepdims=True)
        pv = [jnp.dot(ws[h].astype(BF16), v_ref[:, hsl[h]], preferred_element_type=F32)
              for h in heads]
        for h in heads:
            acc_ref[:, hsl[h]] = acc_ref[:, hsl[h]] + pv[h]

    @pl.when(jj == 0)
    def _():
        step(True)

    @pl.when((jj > 0) & (jj <= i))
    def _():
        step(False)

    @pl.when(jj == i)
    def _():
        o_ref[...] = acc_ref[...].astype(o_ref.dtype)


def _sb_prompt(q, kv, batch, seq, width):
    n_heads = width // HEAD_DIM
    blk = _tile(seq, ATT_BLK, LANES)
    nb = seq // blk
    return pl.pallas_call(
        functools.partial(_sb_prompt_body, n_heads=n_heads), grid=(batch, nb, nb),
        in_specs=[
            pl.BlockSpec((blk, width), lambda b, i, jj: (b * nb + i, 0)),
            pl.BlockSpec((blk, width), lambda b, i, jj: (b * nb + jnp.maximum(i - jj, 0), 0)),
            pl.BlockSpec((blk, width), lambda b, i, jj: (b * nb + jnp.maximum(i - jj, 0), 1)),
        ],
        out_specs=pl.BlockSpec((blk, width), lambda b, i, jj: (b * nb + i, 0)),
        out_shape=jax.ShapeDtypeStruct((batch * seq, width), BF16),
        scratch_shapes=[pltpu.VMEM((blk, width), F32),
                        pltpu.VMEM((n_heads, blk, 1), F32)],
        compiler_params=_cparams(("parallel", "parallel", "arbitrary")),
        name="sb_prompt")(q, kv, kv)


def _sample_attn_body(pt_ref, qa_ref, qc_ref, nkva_ref, nkvc_ref, nlf_ref,
                      cka_ref, cva_ref, clf_ref, ckc_ref, cvc_ref,
                      oa_ref, oc_ref,
                      acca_ref, accc_ref, m_ref, l_ref, cara_ref, carc_ref,
                      *, n_heads, t_new, past_len, page):
    del pt_ref
    step = pl.program_id(1)
    n_steps = pl.num_programs(1)
    rows = n_heads * t_new
    width = n_heads * HEAD_DIM

    @pl.when(step == 0)
    def _():
        acca_ref[...] = jnp.zeros_like(acca_ref)
        accc_ref[...] = jnp.zeros_like(accc_ref)
        m_ref[...] = jnp.full_like(m_ref, NEG_BIG)
        l_ref[...] = jnp.zeros_like(l_ref)
        cara_ref[...] = jnp.zeros_like(cara_ref)
        carc_ref[...] = jnp.zeros_like(carc_ref)

    def scores(q_ref, key_of):
        parts = []
        for h in range(n_heads):
            hs = slice(h * HEAD_DIM, (h + 1) * HEAD_DIM)
            parts.append(lax.dot_general(q_ref[:, hs].astype(BF16), key_of(h),
                                         (((1,), (1,)), ((), ())), preferred_element_type=F32))
        return jnp.concatenate(parts, axis=0)

    def weighted(p, value_of):
        pb = p.astype(BF16)
        return jnp.concatenate(
            [jnp.dot(pb[h * t_new:(h + 1) * t_new, :], value_of(h), preferred_element_type=F32)
             for h in range(n_heads)], axis=0)

    def process(ka_of, va_of, lft, kc_of, vc_of, kbase):
        t_row = lax.broadcasted_iota(jnp.int32, (rows, page), 0) % t_new
        kpos = kbase + lax.broadcasted_iota(jnp.int32, (rows, page), 1)
        qpos = past_len + t_row
        valid_le = kpos <= qpos
        valid_lt = kpos < qpos
        later = _later_matrix(page)

        s = scores(qa_ref, ka_of)
        z = scores(qc_ref, kc_of)
        lfe = jnp.concatenate(
            [jnp.broadcast_to(lft[h:h + 1, :], (t_new, page)) for h in range(n_heads)], axis=0)
        lfm = jnp.where(valid_le, lfe, 0.0)
        log_1mb = -_softplus(z)
        lm = jnp.where(valid_lt, log_1mb, 0.0)
        decay = _split_dot(lfm, later, 3) + cara_ref[...]
        after = _split_dot(lm, later, 2) + carc_ref[...]
        s = jnp.where(valid_le, s + decay, NEG_BIG)
        m_prev = m_ref[...]
        m_new = jnp.maximum(m_prev, jnp.max(s, axis=-1, keepdims=True))
        p = jnp.exp(s - m_new)
        w = jnp.where(valid_lt, jnp.exp(z + log_1mb + after), 0.0)
        a = jnp.exp(m_prev - m_new)
        pv_a = weighted(p, va_of)
        pv_c = weighted(w, vc_of)
        l_ref[...] = a * l_ref[...] + jnp.sum(p, axis=-1, keepdims=True)
        m_ref[...] = m_new
        cara_ref[...] = cara_ref[...] + jnp.sum(lfm, axis=-1, keepdims=True)
        carc_ref[...] = carc_ref[...] + jnp.sum(lm, axis=-1, keepdims=True)
        acca_ref[...] = a * acca_ref[...] + pv_a
        accc_ref[...] = accc_ref[...] + pv_c

    def new_rows(ref, col0):
        def get(h):
            x = ref[:, col0 + h * HEAD_DIM:col0 + (h + 1) * HEAD_DIM]
            pad = jnp.zeros((page - t_new, HEAD_DIM), x.dtype)
            return jnp.concatenate([x, pad], axis=0).astype(BF16)
        return get

    def cache_rows(ref):
        return lambda h: ref[pl.ds(h, page, stride=n_heads), :].astype(BF16)

    @pl.when(step == 0)
    def _():
        process(new_rows(nkva_ref, 0), new_rows(nkva_ref, width), nlf_ref[...],
                new_rows(nkvc_ref, 0), new_rows(nkvc_ref, width), past_len)

    @pl.when(step > 0)
    def _():
        process(cache_rows(cka_ref), cache_rows(cva_ref), clf_ref[...],
                cache_rows(ckc_ref), cache_rows(cvc_ref), (n_steps - 1 - step) * page)

    @pl.when(step == n_steps - 1)
    def _():
        out_a = acca_ref[...] / l_ref[...]
        for h in range(n_heads):
            hs = slice(h * HEAD_DIM, (h + 1) * HEAD_DIM)
            rs = slice(h * t_new, (h + 1) * t_new)
            oa_ref[:, hs] = out_a[rs, :]
            oc_ref[:, hs] = accc_ref[rs, :]


def _sample_attention(page_table, qa, qc, kva32, kvc32, row_blk0, nlf_t, cache_k_a, cache_v_a,
                      cache_lf_t, cache_k_c, cache_v_c, layer, t_new, width):
    dec_batch, n_pages = page_table.shape
    n_heads = width // HEAD_DIM
    page = cache_k_a.shape[2] // n_heads
    rows = n_heads * t_new
    past_len = n_pages * page
    n_steps = n_pages + 1

    new_spec = pl.BlockSpec((t_new, 2 * width), lambda b, s, pt: (row_blk0 + b, 0))

    def page_of(b, s, pt):
        return pt[b * n_pages + n_pages - jnp.maximum(s, 1)]

    def cache_spec(last):
        return pl.BlockSpec((None, None) + last, lambda b, s, pt: (layer, page_of(b, s, pt), 0, 0))

    q_spec = pl.BlockSpec((t_new, width), lambda b, s, pt: (b, 0))
    kernel = functools.partial(_sample_attn_body, n_heads=n_heads, t_new=t_new,
                               past_len=past_len, page=page)
    out = jax.ShapeDtypeStruct((dec_batch * t_new, width), F32)
    kv_page = (page * n_heads, HEAD_DIM)
    return pl.pallas_call(
        kernel,
        grid_spec=pltpu.PrefetchScalarGridSpec(
            num_scalar_prefetch=1, grid=(dec_batch, n_steps),
            in_specs=[q_spec, q_spec, new_spec, new_spec,
                      pl.BlockSpec((None, n_heads, page), lambda b, s, pt: (b, 0, 0)),
                      cache_spec(kv_page), cache_spec(kv_page), cache_spec((n_heads, page)),
                      cache_spec(kv_page), cache_spec(kv_page)],
            out_specs=[q_spec, q_spec],
            scratch_shapes=[pltpu.VMEM((rows, HEAD_DIM), F32), pltpu.VMEM((rows, HEAD_DIM), F32),
                            pltpu.VMEM((rows, 1), F32), pltpu.VMEM((rows, 1), F32),
                            pltpu.VMEM((rows, 1), F32), pltpu.VMEM((rows, 1), F32)]),
        out_shape=(out, out),
        compiler_params=_cparams(("parallel", "arbitrary")), name="sample_attn")(
            page_table.reshape(-1), qa, qc, kva32, kvc32, nlf_t,
            cache_k_a, cache_v_a, cache_lf_t, cache_k_c, cache_v_c)


def _rglru_body(u_ref, g_ref, cw_ref, cb_ref, wa_ref, ba_ref, wx_ref, bx_ref, lam_ref,
                h0_ref, buf_ref, y_ref, hl_ref, nb_ref, xx_ref, h_ref):
    ti = pl.program_id(2)
    nt = pl.num_programs(2)
    tt, cw = u_ref.shape
    hist = SUBLANES
    taps = CONV_WIDTH - 1

    @pl.when(ti == 0)
    def _():
        xx_ref[hist - taps:hist, :] = buf_ref[...]
        h_ref[...] = h0_ref[...]

    xx_ref[hist:hist + tt, :] = u_ref[...]
    xc = cb_ref[...] + sum(
        xx_ref[hist - taps + k:hist - taps + k + tt, :] * cw_ref[k:k + 1, :]
        for k in range(CONV_WIDTH))

    rs, is_ = [], []
    for n in range(cw // LANES):
        cs = slice(n * LANES, (n + 1) * LANES)
        xb = xc[:, cs].astype(BF16)
        rs.append(jnp.dot(xb, wa_ref[n].astype(BF16), preferred_element_type=F32))
        is_.append(jnp.dot(xb, wx_ref[n].astype(BF16), preferred_element_type=F32))
    r = jax.nn.sigmoid(jnp.concatenate(rs, axis=1) + ba_ref[...])
    i = jax.nn.sigmoid(jnp.concatenate(is_, axis=1) + bx_ref[...])
    lam = lam_ref[...]
    softplus_neg_lam = jnp.maximum(-lam, 0.0) + jnp.log1p(jnp.exp(-jnp.abs(lam)))
    log_a = -RG_C * r * softplus_neg_lam
    a = jnp.exp(log_a)
    th = jnp.tanh(log_a)
    bt = jnp.sqrt(-2.0 * th / (1.0 - th)) * (i * xc)

    row = lax.broadcasted_iota(jnp.int32, (tt, cw), 0)
    d = 1
    while d < tt:
        keep = row >= d
        a_sh = jnp.where(keep, pltpu.roll(a, d, 0), 1.0)
        b_sh = jnp.where(keep, pltpu.roll(bt, d, 0), 0.0)
        bt = a * b_sh + bt
        a = a * a_sh
        d *= 2
    h = a * h_ref[...] + bt
    y_ref[...] = (h * jax.nn.gelu(g_ref[...].astype(F32))).astype(y_ref.dtype)
    h_ref[...] = h[tt - 1:tt, :]
    xx_ref[0:hist, :] = xx_ref[tt:tt + hist, :]

    @pl.when(ti == nt - 1)
    def _():
        hl_ref[...] = h[tt - 1:tt, :]
        nb_ref[...] = xx_ref[hist - taps:hist, :]


def _rglru(u_arr, g_arr, batch, t_len, conv_w, conv_b, w_rg_a, b_rg_a, w_rg_x, b_rg_x, lam,
           h0, buf, layer, y_dtype):
    width = lam.shape[-1]
    cw = _tile(width, RG_CW, LANES)
    tt = _tile(t_len, RG_TT, SUBLANES)
    nt = t_len // tt
    nc = width // cw
    nblk = cw // LANES
    assert t_len >= CONV_WIDTH - 1 and tt >= SUBLANES

    tok = pl.BlockSpec((tt, cw), lambda b, c, t: (b * nt + t, c))
    vec = pl.BlockSpec((None, 1, cw), lambda b, c, t: (layer, 0, c))
    gate_w = pl.BlockSpec((None, nblk, LANES, LANES), lambda b, c, t: (layer, c, 0, 0))
    state = pl.BlockSpec((None, 1, cw), lambda b, c, t: (b, 0, c))
    hist = pl.BlockSpec((None, CONV_WIDTH - 1, cw), lambda b, c, t: (b, 0, c))
    return pl.pallas_call(
        _rglru_body, grid=(batch, nc, nt),
        in_specs=[tok, tok,
                  pl.BlockSpec((None, CONV_WIDTH, cw), lambda b, c, t: (layer, 0, c)),
                  vec, gate_w, vec, gate_w, vec, vec, state, hist],
        out_specs=[tok, state, hist],
        out_shape=(jax.ShapeDtypeStruct((batch * t_len, width), y_dtype),
                   jax.ShapeDtypeStruct((batch, 1, width), F32),
                   jax.ShapeDtypeStruct((batch, CONV_WIDTH - 1, width), F32)),
        scratch_shapes=[pltpu.VMEM((tt + 2 * SUBLANES, cw), F32), pltpu.VMEM((1, cw), F32)],
        compiler_params=_cparams(("parallel", "parallel", "arbitrary")), name="rglru")(
            u_arr, g_arr, conv_w, conv_b, w_rg_a, b_rg_a, w_rg_x, b_rg_x, lam, h0, buf)


INFO_E1, INFO_E2, INFO_R1, INFO_R2, INFO_W1, INFO_W2 = range(6)


def _router_body(x_ref, w_ref, b_ref, info_ref, cnt_ref, run_ref, *, n_experts):
    tm = x_ref.shape[0]
    epg = n_experts // N_GROUPS

    @pl.when(pl.program_id(0) == 0)
    def _():
        run_ref[...] = jnp.zeros_like(run_ref)

    logits = jnp.dot(x_ref[...], w_ref[...], preferred_element_type=F32,
                     precision=lax.Precision.HIGHEST) + b_ref[...]
    lane_i = lax.broadcasted_iota(jnp.int32, (tm, LANES), 1)
    lane = lane_i.astype(F32)
    mx = jnp.max(logits, axis=-1, keepdims=True)
    e = jnp.exp(logits - mx)
    probs = e / jnp.sum(e, axis=-1, keepdims=True)

    def top1(vals):
        top = jnp.max(vals, axis=-1, keepdims=True)
        idx = jnp.min(jnp.where(vals == top, lane, float(LANES)), axis=-1, keepdims=True)
        return top, idx

    best = None
    for g in range(N_GROUPS):
        in_g = (lane_i >= g * epg) & (lane_i < (g + 1) * epg)
        pg = jnp.where(in_g, probs, -1.0)
        v1, i1 = top1(pg)
        v2, i2 = top1(jnp.where(lane == i1, -1.0, pg))
        score = v1 + v2
        if best is None:
            best = (score, v1, i1, v2, i2)
        else:
            better = score > best[0]
            best = tuple(jnp.where(better, new, old)
                         for new, old in zip((score, v1, i1, v2, i2), best))
    _, v1, i1, v2, i2 = best
    denom = v1 + v2
    w1 = v1 / denom
    w2 = v2 / denom

    hit1 = lane == i1
    hit2 = lane == i2
    cnt = jnp.where(hit1 | hit2, 1.0, 0.0)
    r = lax.broadcasted_iota(jnp.int32, (tm, tm), 0)
    c = lax.broadcasted_iota(jnp.int32, (tm, tm), 1)
    before = jnp.where(c < r, 1.0, 0.0).astype(BF16)
    prefix = jnp.dot(before, cnt.astype(BF16), preferred_element_type=F32) + run_ref[...]
    r1 = jnp.sum(jnp.where(hit1, prefix, 0.0), axis=-1, keepdims=True)
    r2 = jnp.sum(jnp.where(hit2, prefix, 0.0), axis=-1, keepdims=True)
    run_ref[...] = run_ref[...] + jnp.sum(cnt, axis=0, keepdims=True)

    info = jnp.zeros((tm, LANES), F32)
    for col, val in ((INFO_E1, i1), (INFO_E2, i2), (INFO_R1, r1), (INFO_R2, r2),
                     (INFO_W1, w1), (INFO_W2, w2)):
        info = jnp.where(lane_i == col, val, info)
    info_ref[...] = info
    cnt_ref[...] = run_ref[...]


def _router(x, w_pad, b_pad, n_experts):
    m, d = x.shape
    tm = _tile(m, TOK_TM, ROW_ALIGN)
    return pl.pallas_call(
        functools.partial(_router_body, n_experts=n_experts), grid=(m // tm,),
        in_specs=[pl.BlockSpec((tm, d), lambda i: (i, 0)),
                  pl.BlockSpec((d, LANES), lambda i: (0, 0)),
                  pl.BlockSpec((1, LANES), lambda i: (0, 0))],
        out_specs=[pl.BlockSpec((tm, LANES), lambda i: (i, 0)),
                   pl.BlockSpec((1, LANES), lambda i: (0, 0))],
        out_shape=(jax.ShapeDtypeStruct((m, LANES), F32), jax.ShapeDtypeStruct((1, LANES), F32)),
        scratch_shapes=[pltpu.VMEM((1, LANES), F32)],
        compiler_params=_cparams(("arbitrary",)), name="router")(x, w_pad, b_pad)


def _dispatch_body(pos_ref, x_ref, xs_in_hbm, xs_hbm, sem, *, tm):
    del xs_in_hbm
    base = pl.program_id(0) * tm

    def row_copy(n, slot):
        return pltpu.make_async_copy(x_ref.at[pl.ds(n, 1)],
                                     xs_hbm.at[pl.ds(pos_ref[TOP_K * (base + n) + slot], 1)], sem)

    def start(n, carry):
        for slot in range(TOP_K):
            row_copy(n, slot).start()
        return carry

    def wait(n, carry):
        for slot in range(TOP_K):
            row_copy(n, slot).wait()
        return carry

    lax.fori_loop(0, tm, start, 0)
    lax.fori_loop(0, tm, wait, 0)


def _dispatch(x, pos, n_rows):
    m, d = x.shape
    tm = _tile(m, TOK_TM, ROW_ALIGN)
    xs0 = jnp.zeros((n_rows, d), x.dtype)
    return pl.pallas_call(
        functools.partial(_dispatch_body, tm=tm),
        grid_spec=pltpu.PrefetchScalarGridSpec(
            num_scalar_prefetch=1, grid=(m // tm,),
            in_specs=[pl.BlockSpec((tm, d), lambda i, pos: (i, 0)),
                      pl.BlockSpec(memory_space=pl.ANY)],
            out_specs=pl.BlockSpec(memory_space=pl.ANY),
            scratch_shapes=[pltpu.SemaphoreType.DMA(())]),
        out_shape=jax.ShapeDtypeStruct((n_rows, d), x.dtype),
        input_output_aliases={2: 0},
        compiler_params=_cparams(("arbitrary",)), name="moe_dispatch")(pos, x, xs0)


def _ffn1_body(te_ref, nv_ref, x_ref, wg_ref, wu_ref, h_ref):
    del te_ref

    @pl.when(pl.program_id(0) < nv_ref[0])
    def _():
        xb = x_ref[...].astype(BF16)
        g = jnp.dot(xb, wg_ref[...].astype(BF16), preferred_element_type=F32)
        u = jnp.dot(xb, wu_ref[...].astype(BF16), preferred_element_type=F32)
        h_ref[...] = (g * jax.nn.sigmoid(g) * u).astype(h_ref.dtype)

    @pl.when(pl.program_id(0) >= nv_ref[0])
    def _():
        h_ref[...] = jnp.zeros_like(h_ref)


def _ffn2_body(te_ref, nv_ref, h_ref, wd_ref, y_ref):
    del te_ref

    @pl.when(pl.program_id(0) < nv_ref[0])
    def _():
        y_ref[...] = jnp.dot(h_ref[...], wd_ref[...].astype(BF16), preferred_element_type=F32)

    @pl.when(pl.program_id(0) >= nv_ref[0])
    def _():
        y_ref[...] = jnp.zeros_like(y_ref)


def _expert_ffn(xs, tile_expert, n_valid, w_gate, w_up, w_down, layer):
    n_rows, d = xs.shape
    f = w_gate.shape[-1]
    n_tiles = n_rows // MOE_TM
    row = lambda width: pl.BlockSpec((MOE_TM, width), lambda t, te, nv: (t, 0))
    wspec = lambda a, b: pl.BlockSpec((None, None, a, b), lambda t, te, nv: (layer, te[t], 0, 0))
    hid = pl.pallas_call(
        _ffn1_body,
        grid_spec=pltpu.PrefetchScalarGridSpec(
            num_scalar_prefetch=2, grid=(n_tiles,),
            in_specs=[row(d), wspec(d, f), wspec(d, f)], out_specs=row(f)),
        out_shape=jax.ShapeDtypeStruct((n_rows, f), BF16),
        compiler_params=_cparams(("arbitrary",)), name="moe_ffn1")(
            tile_expert, n_valid, xs, w_gate, w_up)
    return pl.pallas_call(
        _ffn2_body,
        grid_spec=pltpu.PrefetchScalarGridSpec(
            num_scalar_prefetch=2, grid=(n_tiles,),
            in_specs=[row(f), wspec(f, d)], out_specs=row(d)),
        out_shape=jax.ShapeDtypeStruct((n_rows, d), F32),
        compiler_params=_cparams(("arbitrary",)), name="moe_ffn2")(
            tile_expert, n_valid, hid, w_down)


def _combine_body(pos_ref, x_ref, info_ref, g_ref, b_ref, y_hbm, o_ref, obf_ref, ybuf, sem,
                  *, tm, alpha):
    base = pl.program_id(0) * tm

    def row_copy(n, slot):
        return pltpu.make_async_copy(y_hbm.at[pl.ds(pos_ref[TOP_K * (base + n) + slot], 1)],
                                     ybuf.at[slot, pl.ds(n, 1)], sem)

    def start(n, carry):
        for slot in range(TOP_K):
            row_copy(n, slot).start()
        return carry

    def wait(n, carry):
        for slot in range(TOP_K):
            row_copy(n, slot).wait()
        return carry

    lax.fori_loop(0, tm, start, 0)
    lax.fori_loop(0, tm, wait, 0)
    info = info_ref[...]
    moe = info[:, INFO_W1:INFO_W1 + 1] * ybuf[0] + info[:, INFO_W2:INFO_W2 + 1] * ybuf[1]
    out = _layernorm_rows(alpha * x_ref[...] + moe, g_ref[...], b_ref[...])
    o_ref[...] = out
    obf_ref[...] = out.astype(BF16)


def _combine_ln(x, info, pos, y, g, b, layer, alpha):
    m, d = x.shape
    tm = _tile(m, TOK_TM, ROW_ALIGN)
    row = pl.BlockSpec((tm, d), lambda i, pos: (i, 0))
    vec = pl.BlockSpec((None, 1, d), lambda i, pos: (layer, 0, 0))
    return pl.pallas_call(
        functools.partial(_combine_body, tm=tm, alpha=alpha),
        grid_spec=pltpu.PrefetchScalarGridSpec(
            num_scalar_prefetch=1, grid=(m // tm,),
            in_specs=[row, pl.BlockSpec((tm, LANES), lambda i, pos: (i, 0)), vec, vec,
                      pl.BlockSpec(memory_space=pl.ANY)],
            out_specs=[row, row],
            scratch_shapes=[pltpu.VMEM((TOP_K, tm, d), F32), pltpu.SemaphoreType.DMA(())]),
        out_shape=(jax.ShapeDtypeStruct((m, d), F32), jax.ShapeDtypeStruct((m, d), BF16)),
        compiler_params=_cparams(("arbitrary",)), name="moe_combine_ln2")(pos, x, info, g, b, y)


def _moe_plan(info, counts, n_experts, n_tiles):
    e = info[:, INFO_E1:INFO_E2 + 1].astype(jnp.int32)
    rank = info[:, INFO_R1:INFO_R2 + 1].astype(jnp.int32)
    cnt = counts[0, :n_experts].astype(jnp.int32)
    tiles_per = (cnt + MOE_TM - 1) // MOE_TM
    tile_end = jnp.cumsum(tiles_per)
    tile_start = tile_end - tiles_per
    pos = (tile_start * MOE_TM)[e] + rank
    n_valid = tile_end[-1]
    t = jnp.minimum(jnp.arange(n_tiles, dtype=jnp.int32), n_valid - 1)
    tile_expert = jnp.sum(t[:, None] >= tile_end[None, :], axis=1).astype(jnp.int32)
    return pos.reshape(-1), tile_expert, n_valid.reshape(1).astype(jnp.int32)


def kernel(x_prompt, x_sample, cache_k_a, cache_v_a, cache_lf_a, cache_k_c, cache_v_c, state_h_b,
           state_conv_b, page_table, w_in, b_f, conv_w, conv_b, w_rg_a, b_rg_a, w_rg_x, b_rg_x, lam,
           w_proj_a, w_proj_b, w_proj_c, w_out, ln1_g, ln1_b, w_router, b_router, w_e_gate, w_e_up,
           w_e_down, ln2_g, ln2_b):
    batch, seq, d = x_prompt.shape
    dec_batch, t_new, _ = x_sample.shape
    depth = w_in.shape[0]
    d_a = w_proj_a.shape[1]
    w_b = w_proj_b.shape[1]
    d_c = w_proj_c.shape[1]
    h_a = b_f.shape[1]
    n_experts = w_router.shape[1]
    alpha = (2.0 * depth) ** 0.25
    scale = HEAD_DIM ** -0.5
    n_p = batch * seq
    n_s = dec_batch * t_new
    m_all = -(-(n_p + n_s) // ROW_ALIGN) * ROW_ALIGN
    assert d_a == d_c and d_a % MM_TN == 0 and n_p % SUBLANES == 0 and h_a <= SUBLANES

    o_f = 3 * d_a
    o_rest = o_f + h_a
    w_rest = w_in[:, :, o_rest:].astype(BF16)
    r_u, r_g = 0, w_b
    r_qc = 2 * w_b
    r_kvc = r_qc + d_c
    r_gl = r_qc + 3 * d_c
    bias_f = jnp.concatenate([b_f, jnp.zeros((depth, LANES - h_a), b_f.dtype)], axis=1)[:, None, :]

    vec3 = lambda a: a.reshape(depth, 1, a.shape[-1])
    conv_b3, lam3 = vec3(conv_b), vec3(lam)
    b_rg_a3 = b_rg_a.reshape(depth, 1, w_b)
    b_rg_x3 = b_rg_x.reshape(depth, 1, w_b)
    ln1_g3, ln1_b3, ln2_g3, ln2_b3 = vec3(ln1_g), vec3(ln1_b), vec3(ln2_g), vec3(ln2_b)
    w_r_pad = jnp.concatenate([w_router, jnp.zeros((d, LANES - n_experts), w_router.dtype)], axis=1)
    b_r_pad = jnp.concatenate([b_router, jnp.full((LANES - n_experts,), NEG_BIG, b_router.dtype)])[None, :]

    n_pool, page = cache_k_a.shape[1], cache_k_a.shape[2]
    rows_of = lambda c: c.reshape(depth, n_pool, page * c.shape[3], HEAD_DIM)
    ck_a, cv_a, ck_c, cv_c = rows_of(cache_k_a), rows_of(cache_v_a), rows_of(cache_k_c), rows_of(cache_v_c)
    clf_t = jnp.swapaxes(cache_lf_a, 2, 3)

    zeros_h = jnp.zeros((batch, 1, w_b), F32)
    zeros_buf = jnp.zeros((batch, CONV_WIDTH - 1, w_b), F32)
    n_tiles = (TOP_K * m_all + n_experts * (MOE_TM - 1) + MOE_TM - 1) // MOE_TM

    x = jnp.concatenate([x_prompt.reshape(n_p, d), x_sample.reshape(n_s, d),
                         jnp.zeros((m_all - n_p - n_s, d), F32)], axis=0)
    x_bf = x.astype(BF16)
    pad_rows = lambda a: jnp.concatenate(
        [a, jnp.zeros((m_all - a.shape[0], a.shape[1]), a.dtype)], axis=0)
    sample = lambda a: a[n_p:n_p + n_s]

    new_p = [[] for _ in range(7)]
    new_s = [[] for _ in range(7)]
    for l in range(depth):
        (qa16,) = _matmul(x_bf, w_in, l, 0, d_a, (BF16,), epilogue="scale", alpha=scale,
                          name="inproj_qa")
        kva32, kva16 = _matmul(x_bf, w_in, l, d_a, 2 * d_a, (F32, BF16), name="inproj_kva")
        (lf,) = _matmul(x_bf, w_in, l, o_f, LANES, (F32,), tn=LANES, epilogue="logsig_bias",
                        extra=bias_f, name="inproj_forget")
        (u32,) = _matmul(x_bf, w_rest, l, r_u, w_b, (F32,), name="inproj_u")
        (g16,) = _matmul(x_bf, w_rest, l, r_g, w_b, (BF16,), name="inproj_g")
        (qc16,) = _matmul(x_bf, w_rest, l, r_qc, d_c, (BF16,), epilogue="scale", alpha=scale,
                          name="inproj_qc")
        kvc32, kvc16 = _matmul(x_bf, w_rest, l, r_kvc, 2 * d_c, (F32, BF16), name="inproj_kvc")
        (gates,) = _matmul(x_bf, w_rest, l, r_gl, 3 * d, (BF16,), epilogue="sigmoid",
                           name="inproj_gates")

        c_col = _prompt_cumsum(lf, batch, seq)
        c_row = jnp.swapaxes(c_col.reshape(batch, seq, LANES)[:, :, :SUBLANES], 1, 2)
        ya_p = _fox_prompt(qa16, kva16, c_col, c_row, batch, seq, d_a)
        yc_p = _sb_prompt(qc16, kvc16, batch, seq, d_c)
        yb_p, hl_p, nb_p = _rglru(u32, g16, batch, seq, conv_w, conv_b3, w_rg_a, b_rg_a3,
                                  w_rg_x, b_rg_x3, lam3, zeros_h, zeros_buf, l, BF16)

        lf_s = sample(lf)[:, :SUBLANES].reshape(dec_batch, t_new, SUBLANES)
        nlf_t = jnp.concatenate([jnp.swapaxes(lf_s, 1, 2),
                                 jnp.zeros((dec_batch, SUBLANES, page - t_new), F32)], axis=2)
        ya_s, yc_s = _sample_attention(page_table, sample(qa16).astype(F32),
                                       sample(qc16).astype(F32), kva32, kvc32, n_p // t_new,
                                       nlf_t, ck_a, cv_a, clf_t, ck_c, cv_c, l, t_new, d_a)
        yb_s, hl_s, nb_s = _rglru(sample(u32), sample(g16).astype(F32), dec_batch, t_new,
                                  conv_w, conv_b3, w_rg_a, b_rg_a3, w_rg_x, b_rg_x3, lam3,
                                  state_h_b[l][:, None, :], state_conv_b[l], l, F32)

        ya = pad_rows(jnp.concatenate([ya_p, ya_s.astype(BF16)], axis=0))
        yb = pad_rows(jnp.concatenate([yb_p, yb_s.astype(BF16)], axis=0))
        yc = pad_rows(jnp.concatenate([yc_p, yc_s.astype(BF16)], axis=0))
        merged = _merge(ya, yb, yc, w_proj_a, w_proj_b, w_proj_c, gates, l)
        (pre1,) = _matmul(merged, w_out, l, 0, d, (F32,), epilogue="residual", extra=x,
                          alpha=alpha, name="out_proj")
        x1 = _layernorm(pre1, ln1_g3, ln1_b3, l)

        info, counts = _router(x1, w_r_pad, b_r_pad, n_experts)
        pos, tile_expert, n_valid = _moe_plan(info, counts, n_experts, n_tiles)
        xs = _dispatch(x1, pos, n_tiles * MOE_TM)
        y = _expert_ffn(xs, tile_expert, n_valid, w_e_gate, w_e_up, w_e_down, l)
        x, x_bf = _combine_ln(x1, info, pos, y, ln2_g3, ln2_b3, l, alpha)

        def heads(a, rows0, nrows, col0, lead):
            return a[rows0:rows0 + nrows, col0:col0 + d_a].reshape(lead + (d_a // HEAD_DIM, HEAD_DIM))

        for dst, rows0, nrows, lead in ((new_p, 0, n_p, (batch, seq)),
                                        (new_s, n_p, n_s, (dec_batch, t_new))):
            dst[0].append(heads(kva32, rows0, nrows, 0, lead))
            dst[1].append(heads(kva32, rows0, nrows, d_a, lead))
            dst[2].append(lf[rows0:rows0 + nrows, :h_a].reshape(lead + (h_a,)))
            dst[3].append(heads(kvc32, rows0, nrows, 0, lead))
            dst[4].append(heads(kvc32, rows0, nrows, d_c, lead))
        new_p[5].append(hl_p[:, 0, :])
        new_p[6].append(nb_p)
        new_s[5].append(hl_s[:, 0, :])
        new_s[6].append(nb_s)

    y_prompt = x[:n_p].reshape(batch, seq, d)
    y_sample = x[n_p:n_p + n_s].reshape(dec_batch, t_new, d)
    return (y_prompt, y_sample) + tuple(jnp.stack(s) for s in new_p) + tuple(jnp.stack(s) for s in new_s)
```

```python
import functools

import jax
import jax.numpy as jnp
from jax import lax
from jax.experimental import pallas as pl
from jax.experimental.pallas import tpu as pltpu

F32 = jnp.float32
BF16 = jnp.bfloat16

LANES = 128
SUBLANES = 8
VMEM_LIMIT_BYTES = 56 * 1024 * 1024

HEAD_DIM = 128
CONV_WIDTH = 4
RG_C = 8.0
N_GROUPS = 4
TOP_K = 2
LN_EPS = 1e-5
NEG_BIG = -1e30

ROW_ALIGN = 256
MM_TM = 768
MM_TN = 512
MM_TN_BF16 = 1024
ATT_BLK = 256
RG_TT = 256
RG_CW = 512
MOE_TM = 256
TOK_TM = 256


def _cparams(sem):
    return pltpu.CompilerParams(dimension_semantics=sem, vmem_limit_bytes=VMEM_LIMIT_BYTES)


def _tile(n, pref, align):
    if n <= pref:
        return n
    t = (pref // align) * align
    while t >= align:
        if n % t == 0:
            return t
        t -= align
    raise ValueError(f"no tile for {n} (pref {pref}, align {align})")


def _softplus(x):
    return jnp.maximum(x, 0.0) + jnp.log(1.0 + jnp.exp(-jnp.abs(x)))


def _log_sigmoid(x):
    return jnp.minimum(x, 0.0) - jnp.log1p(jnp.exp(-jnp.abs(x)))


def _split_dot(a, t_bf16, parts):
    acc = None
    r = a
    for p in range(parts):
        piece = r.astype(BF16)
        d = jnp.dot(piece, t_bf16, preferred_element_type=F32)
        acc = d if acc is None else acc + d
        if p + 1 < parts:
            r = r - piece.astype(F32)
    return acc


def _later_matrix(n):
    r = lax.broadcasted_iota(jnp.int32, (n, n), 0)
    c = lax.broadcasted_iota(jnp.int32, (n, n), 1)
    return jnp.where(r > c, 1.0, 0.0).astype(BF16)


def _mm_body(*refs, epilogue, alpha):
    x_ref, w_ref = refs[0], refs[1]
    rest = refs[2:]
    acc = jnp.dot(x_ref[...], w_ref[...].astype(BF16), preferred_element_type=F32)
    if epilogue == "scale":
        acc = alpha * acc
    elif epilogue == "sigmoid":
        acc = jax.nn.sigmoid(acc)
    elif epilogue == "logsig_bias":
        acc = _log_sigmoid(acc + rest[0][...])
        rest = rest[1:]
    elif epilogue == "residual":
        acc = alpha * rest[0][...] + acc
        rest = rest[1:]
    for o_ref in rest:
        o_ref[...] = acc.astype(o_ref.dtype)


def _matmul(x, w, layer, col0, n_cols, out_dtypes, *, tn=MM_TN, epilogue="none", extra=None,
            alpha=1.0, name):
    m, k = x.shape
    tm = _tile(m, MM_TM, ROW_ALIGN)
    assert col0 % tn == 0 and n_cols % tn == 0
    col_blk0 = col0 // tn
    grid = (m // tm, n_cols // tn)
    in_specs = [
        pl.BlockSpec((tm, k), lambda i, j: (i, 0)),
        pl.BlockSpec((None, k, tn), lambda i, j: (layer, 0, j + col_blk0)),
    ]
    args = [x, w]
    if epilogue == "logsig_bias":
        in_specs.append(pl.BlockSpec((None, 1, tn), lambda i, j: (layer, 0, j)))
        args.append(extra)
    elif epilogue == "residual":
        in_specs.append(pl.BlockSpec((tm, tn), lambda i, j: (i, j)))
        args.append(extra)
    outs = tuple(jax.ShapeDtypeStruct((m, n_cols), dt) for dt in out_dtypes)
    out_specs = tuple(pl.BlockSpec((tm, tn), lambda i, j: (i, j)) for _ in out_dtypes)
    return pl.pallas_call(
        functools.partial(_mm_body, epilogue=epilogue, alpha=alpha),
        grid=grid, in_specs=in_specs, out_specs=out_specs, out_shape=outs,
        compiler_params=_cparams(("parallel", "arbitrary")), name=name)(*args)


def _rebase_body(a_ref, b_ref, o_ref, *, shift):
    o_ref[...] = jnp.concatenate([a_ref[:, shift:], b_ref[:, :shift]], axis=1).astype(o_ref.dtype)


def _rebase_columns(w, col0):
    depth, k, n = w.shape
    shift = col0 % LANES
    base = col0 - shift
    n_out = n - col0
    tn = MM_TN
    tk = _tile(k, 1024, SUBLANES)
    assert 0 < shift and base % tn == 0 and n_out % tn == 0
    return pl.pallas_call(
        functools.partial(_rebase_body, shift=shift), grid=(depth, k // tk, n_out // tn),
        in_specs=[pl.BlockSpec((None, tk, tn), lambda l, r, j: (l, r, base // tn + j)),
                  pl.BlockSpec((None, tk, LANES),
                               lambda l, r, j: (l, r, (base + (j + 1) * tn) // LANES))],
        out_specs=pl.BlockSpec((None, tk, tn), lambda l, r, j: (l, r, j)),
        out_shape=jax.ShapeDtypeStruct((depth, k, n_out), BF16),
        compiler_params=_cparams(("parallel", "parallel", "arbitrary")),
        name="rebase_w_in")(w, w)


def _merge_body(ya, yb, yc, wa, wb, wc, g0, g1, g2, o_ref):
    pa = jnp.dot(ya[...], wa[...].astype(BF16), preferred_element_type=F32)
    pb = jnp.dot(yb[...], wb[...].astype(BF16), preferred_element_type=F32)
    pc = jnp.dot(yc[...], wc[...].astype(BF16), preferred_element_type=F32)
    m = g0[...].astype(F32) * pa + g1[...].astype(F32) * pb + g2[...].astype(F32) * pc
    o_ref[...] = m.astype(o_ref.dtype)


def _merge(ya, yb, yc, w_proj_a, w_proj_b, w_proj_c, gates, layer):
    m = ya.shape[0]
    d = w_proj_a.shape[-1]
    tm = _tile(m, MM_TM, ROW_ALIGN)
    tn = MM_TN
    nb = d // tn

    def act(width):
        return pl.BlockSpec((tm, width), lambda i, j: (i, 0))

    def wgt(width):
        return pl.BlockSpec((None, width, tn), lambda i, j: (layer, 0, j))

    def gate(branch):
        return pl.BlockSpec((tm, tn), lambda i, j: (i, j + branch * nb))

    return pl.pallas_call(
        _merge_body, grid=(m // tm, nb),
        in_specs=[act(ya.shape[1]), act(yb.shape[1]), act(yc.shape[1]),
                  wgt(ya.shape[1]), wgt(yb.shape[1]), wgt(yc.shape[1]),
                  gate(0), gate(1), gate(2)],
        out_specs=pl.BlockSpec((tm, tn), lambda i, j: (i, j)),
        out_shape=jax.ShapeDtypeStruct((m, d), BF16),
        compiler_params=_cparams(("parallel", "arbitrary")), name="merge")(
            ya, yb, yc, w_proj_a, w_proj_b, w_proj_c, gates, gates, gates)


def _layernorm_rows(x, g, b):
    mu = jnp.mean(x, axis=-1, keepdims=True)
    xc = x - mu
    var = jnp.mean(xc * xc, axis=-1, keepdims=True)
    return xc * lax.rsqrt(var + LN_EPS) * g + b


def _ln_body(x_ref, g_ref, b_ref, o_ref):
    o_ref[...] = _layernorm_rows(x_ref[...], g_ref[...], b_ref[...])


def _layernorm(x, g, b, layer):
    m, d = x.shape
    tm = _tile(m, TOK_TM, ROW_ALIGN)
    vec = pl.BlockSpec((None, 1, d), lambda i: (layer, 0, 0))
    row = pl.BlockSpec((tm, d), lambda i: (i, 0))
    return pl.pallas_call(
        _ln_body, grid=(m // tm,), in_specs=[row, vec, vec], out_specs=row,
        out_shape=jax.ShapeDtypeStruct((m, d), F32),
        compiler_params=_cparams(("parallel",)), name="ln1")(x, g, b)


def _cumsum_body(lf_ref, c_ref, carry_ref):
    ts = lf_ref.shape[0]

    @pl.when(pl.program_id(1) == 0)
    def _():
        carry_ref[...] = jnp.zeros_like(carry_ref)

    r = lax.broadcasted_iota(jnp.int32, (ts, ts), 0)
    c = lax.broadcasted_iota(jnp.int32, (ts, ts), 1)
    lower = jnp.where(c <= r, 1.0, 0.0).astype(BF16)
    acc = None
    rem = lf_ref[...]
    for p in range(3):
        piece = rem.astype(BF16)
        dd = jnp.dot(lower, piece, preferred_element_type=F32)
        acc = dd if acc is None else acc + dd
        if p < 2:
            rem = rem - piece.astype(F32)
    out = acc + carry_ref[...]
    c_ref[...] = out
    carry_ref[...] = out[ts - 1:ts, :]


def _prompt_cumsum(lf, batch, seq):
    ts = _tile(seq, ATT_BLK, SUBLANES)
    nt = seq // ts
    spec = pl.BlockSpec((ts, LANES), lambda b, t: (b * nt + t, 0))
    return pl.pallas_call(
        _cumsum_body, grid=(batch, nt), in_specs=[spec], out_specs=spec,
        out_shape=jax.ShapeDtypeStruct((batch * seq, LANES), F32),
        scratch_shapes=[pltpu.VMEM((1, LANES), F32)],
        compiler_params=_cparams(("parallel", "arbitrary")), name="lf_cumsum")(lf)


def _fox_prompt_body(q_ref, k_ref, v_ref, ccol_ref, crow_ref, o_ref, acc_ref, m_ref, l_ref,
                     *, n_heads):
    i = pl.program_id(1)
    j = pl.program_id(2)
    tq = q_ref.shape[0]
    tk = k_ref.shape[0]

    @pl.when(j == 0)
    def _():
        acc_ref[...] = jnp.zeros_like(acc_ref)
        m_ref[...] = jnp.full_like(m_ref, NEG_BIG)
        l_ref[...] = jnp.zeros_like(l_ref)

    def step(diagonal):
        if diagonal:
            mask = (lax.broadcasted_iota(jnp.int32, (tq, tk), 1)
                    <= lax.broadcasted_iota(jnp.int32, (tq, tk), 0))
        heads = range(n_heads)
        hsl = [slice(h * HEAD_DIM, (h + 1) * HEAD_DIM) for h in heads]
        ss = [lax.dot_general(q_ref[:, hsl[h]], k_ref[:, hsl[h]], (((1,), (1,)), ((), ())),
                              preferred_element_type=F32)
              - (crow_ref[h:h + 1, :] - ccol_ref[0:1, h:h + 1]) for h in heads]
        if diagonal:
            ss = [jnp.where(mask, s, NEG_BIG) for s in ss]
        m_prev = [m_ref[h] for h in heads]
        m_new = [jnp.maximum(m_prev[h], jnp.max(ss[h], axis=-1, keepdims=True)) for h in heads]
        ps = [jnp.exp(ss[h] - m_new[h]) for h in heads]
        al = [jnp.exp(m_prev[h] - m_new[h]) for h in heads]
        for h in heads:
            l_ref[h] = al[h] * l_ref[h] + jnp.sum(ps[h], axis=-1, keepdims=True)
            m_ref[h] = m_new[h]
        pv = [jnp.dot(ps[h].astype(BF16), v_ref[:, hsl[h]], preferred_element_type=F32)
              for h in heads]
        for h in heads:
            acc_ref[:, hsl[h]] = al[h] * acc_ref[:, hsl[h]] + pv[h]

    @pl.when(j < i)
    def _():
        step(False)

    @pl.when(j == i)
    def _():
        step(True)
        for h in range(n_heads):
            hs = slice(h * HEAD_DIM, (h + 1) * HEAD_DIM)
            o_ref[:, hs] = (acc_ref[:, hs] / l_ref[h]).astype(o_ref.dtype)


def _fox_prompt(q, kv, c_col, c_row, batch, seq, width):
    n_heads = width // HEAD_DIM
    blk = _tile(seq, ATT_BLK, LANES)
    nb = seq // blk
    return pl.pallas_call(
        functools.partial(_fox_prompt_body, n_heads=n_heads), grid=(batch, nb, nb),
        in_specs=[
            pl.BlockSpec((blk, width), lambda b, i, j: (b * nb + i, 0)),
            pl.BlockSpec((blk, width), lambda b, i, j: (b * nb + jnp.minimum(j, i), 0)),
            pl.BlockSpec((blk, width), lambda b, i, j: (b * nb + jnp.minimum(j, i), 1)),
            pl.BlockSpec((blk, LANES), lambda b, i, j: (b * nb + i, 0)),
            pl.BlockSpec((None, SUBLANES, blk), lambda b, i, j: (b, 0, jnp.minimum(j, i))),
        ],
        out_specs=pl.BlockSpec((blk, width), lambda b, i, j: (b * nb + i, 0)),
        out_shape=jax.ShapeDtypeStruct((batch * seq, width), BF16),
        scratch_shapes=[pltpu.VMEM((blk, width), F32),
                        pltpu.VMEM((n_heads, blk, 1), F32),
                        pltpu.VMEM((n_heads, blk, 1), F32)],
        compiler_params=_cparams(("parallel", "parallel", "arbitrary")),
        name="fox_prompt")(q, kv, kv, c_col, c_row)


def _sb_prompt_body(q_ref, k_ref, v_ref, o_ref, acc_ref, carry_ref, *, n_heads):
    i = pl.program_id(1)
    jj = pl.program_id(2)
    tq = q_ref.shape[0]
    tk = k_ref.shape[0]

    @pl.when(jj == 0)
    def _():
        acc_ref[...] = jnp.zeros_like(acc_ref)
        carry_ref[...] = jnp.zeros_like(carry_ref)

    def step(diagonal):
        later = _later_matrix(tk)
        if diagonal:
            mask = (lax.broadcasted_iota(jnp.int32, (tq, tk), 1)
                    < lax.broadcasted_iota(jnp.int32, (tq, tk), 0))
        heads = range(n_heads)
        hsl = [slice(h * HEAD_DIM, (h + 1) * HEAD_DIM) for h in heads]
        zs = [lax.dot_general(q_ref[:, hsl[h]], k_ref[:, hsl[h]], (((1,), (1,)), ((), ())),
                              preferred_element_type=F32) for h in heads]
        log_1mb = [-_softplus(z) for z in zs]
        lms = [jnp.where(mask, x, 0.0) for x in log_1mb] if diagonal else log_1mb
        after = [_split_dot(lms[h], later, 2) + carry_ref[h] for h in heads]
        ws = [jnp.exp(zs[h] + log_1mb[h] + after[h]) for h in heads]
        if diagonal:
            ws = [jnp.where(mask, w, 0.0) for w in ws]
        for h in heads:
            carry_ref[h] = carry_ref[h] + jnp.sum(lms[h], axis=-1, keepdims=True)
        pv = [jnp.dot(ws[h].astype(BF16), v_ref[:, hsl[h]], preferred_element_type=F32)
              for h in heads]
        for h in heads:
            acc_ref[:, hsl[h]] = acc_ref[:, hsl[h]] + pv[h]

    @pl.when(jj == 0)
    def _():
        step(True)

    @pl.when((jj > 0) & (jj <= i))
    def _():
        step(False)

    @pl.when(jj == i)
    def _():
        o_ref[...] = acc_ref[...].astype(o_ref.dtype)


def _sb_prompt(q, kv, batch, seq, width):
    n_heads = width // HEAD_DIM
    blk = _tile(seq, ATT_BLK, LANES)
    nb = seq // blk
    return pl.pallas_call(
        functools.partial(_sb_prompt_body, n_heads=n_heads), grid=(batch, nb, nb),
        in_specs=[
            pl.BlockSpec((blk, width), lambda b, i, jj: (b * nb + i, 0)),
            pl.BlockSpec((blk, width), lambda b, i, jj: (b * nb + jnp.maximum(i - jj, 0), 0)),
            pl.BlockSpec((blk, width), lambda b, i, jj: (b * nb + jnp.maximum(i - jj, 0), 1)),
        ],
        out_specs=pl.BlockSpec((blk, width), lambda b, i, jj: (b * nb + i, 0)),
        out_shape=jax.ShapeDtypeStruct((batch * seq, width), BF16),
        scratch_shapes=[pltpu.VMEM((blk, width), F32),
                        pltpu.VMEM((n_heads, blk, 1), F32)],
        compiler_params=_cparams(("parallel", "parallel", "arbitrary")),
        name="sb_prompt")(q, kv, kv)


def _sample_attn_body(pt_ref, qa_ref, qc_ref, nkva_ref, nkvc_ref, nlf_ref,
                      cka_ref, cva_ref, clf_ref, ckc_ref, cvc_ref,
                      oa_ref, oc_ref,
                      acca_ref, accc_ref, m_ref, l_ref, cara_ref, carc_ref,
                      *, n_heads, t_new, past_len, page):
    del pt_ref
    step = pl.program_id(1)
    n_steps = pl.num_programs(1)
    rows = n_heads * t_new
    width = n_heads * HEAD_DIM

    @pl.when(step == 0)
    def _():
        acca_ref[...] = jnp.zeros_like(acca_ref)
        accc_ref[...] = jnp.zeros_like(accc_ref)
        m_ref[...] = jnp.full_like(m_ref, NEG_BIG)
        l_ref[...] = jnp.zeros_like(l_ref)
        cara_ref[...] = jnp.zeros_like(cara_ref)
        carc_ref[...] = jnp.zeros_like(carc_ref)

    def scores(q_ref, key_of):
        parts = []
        for h in range(n_heads):
            hs = slice(h * HEAD_DIM, (h + 1) * HEAD_DIM)
            parts.append(lax.dot_general(q_ref[:, hs].astype(BF16), key_of(h),
                                         (((1,), (1,)), ((), ())), preferred_element_type=F32))
        return jnp.concatenate(parts, axis=0)

    def weighted(p, value_of):
        pb = p.astype(BF16)
        return jnp.concatenate(
            [jnp.dot(pb[h * t_new:(h + 1) * t_new, :], value_of(h), preferred_element_type=F32)
             for h in range(n_heads)], axis=0)

    def process(ka_of, va_of, lft, kc_of, vc_of, kbase):
        t_row = lax.broadcasted_iota(jnp.int32, (rows, page), 0) % t_new
        kpos = kbase + lax.broadcasted_iota(jnp.int32, (rows, page), 1)
        qpos = past_len + t_row
        valid_le = kpos <= qpos
        valid_lt = kpos < qpos
        later = _later_matrix(page)

        s = scores(qa_ref, ka_of)
        z = scores(qc_ref, kc_of)
        lfe = jnp.concatenate(
            [jnp.broadcast_to(lft[h:h + 1, :], (t_new, page)) for h in range(n_heads)], axis=0)
        lfm = jnp.where(valid_le, lfe, 0.0)
        log_1mb = -_softplus(z)
        lm = jnp.where(valid_lt, log_1mb, 0.0)
        decay = _split_dot(lfm, later, 3) + cara_ref[...]
        after = _split_dot(lm, later, 2) + carc_ref[...]
        s = jnp.where(valid_le, s + decay, NEG_BIG)
        m_prev = m_ref[...]
        m_new = jnp.maximum(m_prev, jnp.max(s, axis=-1, keepdims=True))
        p = jnp.exp(s - m_new)
        w = jnp.where(valid_lt, jnp.exp(z + log_1mb + after), 0.0)
        a = jnp.exp(m_prev - m_new)
        pv_a = weighted(p, va_of)
        pv_c = weighted(w, vc_of)
        l_ref[...] = a * l_ref[...] + jnp.sum(p, axis=-1, keepdims=True)
        m_ref[...] = m_new
        cara_ref[...] = cara_ref[...] + jnp.sum(lfm, axis=-1, keepdims=True)
        carc_ref[...] = carc_ref[...] + jnp.sum(lm, axis=-1, keepdims=True)
        acca_ref[...] = a * acca_ref[...] + pv_a
        accc_ref[...] = accc_ref[...] + pv_c

    def new_rows(ref, col0):
        def get(h):
            x = ref[:, col0 + h * HEAD_DIM:col0 + (h + 1) * HEAD_DIM]
            pad = jnp.zeros((page - t_new, HEAD_DIM), x.dtype)
            return jnp.concatenate([x, pad], axis=0).astype(BF16)
        return get

    def cache_rows(ref):
        return lambda h: ref[pl.ds(h, page, stride=n_heads), :].astype(BF16)

    @pl.when(step == 0)
    def _():
        process(new_rows(nkva_ref, 0), new_rows(nkva_ref, width), nlf_ref[...],
                new_rows(nkvc_ref, 0), new_rows(nkvc_ref, width), past_len)

    @pl.when(step > 0)
    def _():
        process(cache_rows(cka_ref), cache_rows(cva_ref), clf_ref[...],
                cache_rows(ckc_ref), cache_rows(cvc_ref), (n_steps - 1 - step) * page)

    @pl.when(step == n_steps - 1)
    def _():
        out_a = acca_ref[...] / l_ref[...]
        for h in range(n_heads):
            hs = slice(h * HEAD_DIM, (h + 1) * HEAD_DIM)
            rs = slice(h * t_new, (h + 1) * t_new)
            oa_ref[:, hs] = out_a[rs, :]
            oc_ref[:, hs] = accc_ref[rs, :]


def _sample_attention(page_table, qa, qc, kva32, kvc32, row_blk0, nlf_t, cache_k_a, cache_v_a,
                      cache_lf_t, cache_k_c, cache_v_c, layer, t_new, width):
    dec_batch, n_pages = page_table.shape
    n_heads = width // HEAD_DIM
    page = cache_k_a.shape[2] // n_heads
    rows = n_heads * t_new
    past_len = n_pages * page
    n_steps = n_pages + 1

    new_spec = pl.BlockSpec((t_new, 2 * width), lambda b, s, pt: (row_blk0 + b, 0))

    def page_of(b, s, pt):
        return pt[b * n_pages + n_pages - jnp.maximum(s, 1)]

    def cache_spec(last):
        return pl.BlockSpec((None, None) + last, lambda b, s, pt: (layer, page_of(b, s, pt), 0, 0))

    q_spec = pl.BlockSpec((t_new, width), lambda b, s, pt: (b, 0))
    kernel = functools.partial(_sample_attn_body, n_heads=n_heads, t_new=t_new,
                               past_len=past_len, page=page)
    out = jax.ShapeDtypeStruct((dec_batch * t_new, width), F32)
    kv_page = (page * n_heads, HEAD_DIM)
    return pl.pallas_call(
        kernel,
        grid_spec=pltpu.PrefetchScalarGridSpec(
            num_scalar_prefetch=1, grid=(dec_batch, n_steps),
            in_specs=[q_spec, q_spec, new_spec, new_spec,
                      pl.BlockSpec((None, n_heads, page), lambda b, s, pt: (b, 0, 0)),
                      cache_spec(kv_page), cache_spec(kv_page), cache_spec((n_heads, page)),
                      cache_spec(kv_page), cache_spec(kv_page)],
            out_specs=[q_spec, q_spec],
            scratch_shapes=[pltpu.VMEM((rows, HEAD_DIM), F32), pltpu.VMEM((rows, HEAD_DIM), F32),
                            pltpu.VMEM((rows, 1), F32), pltpu.VMEM((rows, 1), F32),
                            pltpu.VMEM((rows, 1), F32), pltpu.VMEM((rows, 1), F32)]),
        out_shape=(out, out),
        compiler_params=_cparams(("parallel", "arbitrary")), name="sample_attn")(
            page_table.reshape(-1), qa, qc, kva32, kvc32, nlf_t,
            cache_k_a, cache_v_a, cache_lf_t, cache_k_c, cache_v_c)


def _rglru_body(u_ref, g_ref, cw_ref, cb_ref, wa_ref, ba_ref, wx_ref, bx_ref, lam_ref,
                h0_ref, buf_ref, y_ref, hl_ref, nb_ref, xx_ref, h_ref):
    ti = pl.program_id(2)
    nt = pl.num_programs(2)
    tt, cw = u_ref.shape
    hist = SUBLANES
    taps = CONV_WIDTH - 1

    @pl.when(ti == 0)
    def _():
        xx_ref[hist - taps:hist, :] = buf_ref[...]
        h_ref[...] = h0_ref[...]

    xx_ref[hist:hist + tt, :] = u_ref[...]
    xc = cb_ref[...] + sum(
        xx_ref[hist - taps + k:hist - taps + k + tt, :] * cw_ref[k:k + 1, :]
        for k in range(CONV_WIDTH))

    rs, is_ = [], []
    for n in range(cw // LANES):
        cs = slice(n * LANES, (n + 1) * LANES)
        xb = xc[:, cs].astype(BF16)
        rs.append(jnp.dot(xb, wa_ref[n].astype(BF16), preferred_element_type=F32))
        is_.append(jnp.dot(xb, wx_ref[n].astype(BF16), preferred_element_type=F32))
    r = jax.nn.sigmoid(jnp.concatenate(rs, axis=1) + ba_ref[...])
    i = jax.nn.sigmoid(jnp.concatenate(is_, axis=1) + bx_ref[...])
    lam = lam_ref[...]
    softplus_neg_lam = jnp.maximum(-lam, 0.0) + jnp.log1p(jnp.exp(-jnp.abs(lam)))
    log_a = -RG_C * r * softplus_neg_lam
    a = jnp.exp(log_a)
    th = jnp.tanh(log_a)
    bt = jnp.sqrt(-2.0 * th / (1.0 - th)) * (i * xc)

    row = lax.broadcasted_iota(jnp.int32, (tt, cw), 0)
    d = 1
    while d < tt:
        keep = row >= d
        a_sh = jnp.where(keep, pltpu.roll(a, d, 0), 1.0)
        b_sh = jnp.where(keep, pltpu.roll(bt, d, 0), 0.0)
        bt = a * b_sh + bt
        a = a * a_sh
        d *= 2
    h = a * h_ref[...] + bt
    y_ref[...] = (h * jax.nn.gelu(g_ref[...].astype(F32))).astype(y_ref.dtype)
    h_ref[...] = h[tt - 1:tt, :]
    xx_ref[0:hist, :] = xx_ref[tt:tt + hist, :]

    @pl.when(ti == nt - 1)
    def _():
        hl_ref[...] = h[tt - 1:tt, :]
        nb_ref[...] = xx_ref[hist - taps:hist, :]


def _rglru(u_arr, g_arr, batch, t_len, conv_w, conv_b, w_rg_a, b_rg_a, w_rg_x, b_rg_x, lam,
           h0, buf, layer, y_dtype):
    width = lam.shape[-1]
    cw = _tile(width, RG_CW, LANES)
    tt = _tile(t_len, RG_TT, SUBLANES)
    nt = t_len // tt
    nc = width // cw
    nblk = cw // LANES
    assert t_len >= CONV_WIDTH - 1 and tt >= SUBLANES

    tok = pl.BlockSpec((tt, cw), lambda b, c, t: (b * nt + t, c))
    vec = pl.BlockSpec((None, 1, cw), lambda b, c, t: (layer, 0, c))
    gate_w = pl.BlockSpec((None, nblk, LANES, LANES), lambda b, c, t: (layer, c, 0, 0))
    state = pl.BlockSpec((None, 1, cw), lambda b, c, t: (b, 0, c))
    hist = pl.BlockSpec((None, CONV_WIDTH - 1, cw), lambda b, c, t: (b, 0, c))
    return pl.pallas_call(
        _rglru_body, grid=(batch, nc, nt),
        in_specs=[tok, tok,
                  pl.BlockSpec((None, CONV_WIDTH, cw), lambda b, c, t: (layer, 0, c)),
                  vec, gate_w, vec, gate_w, vec, vec, state, hist],
        out_specs=[tok, state, hist],
        out_shape=(jax.ShapeDtypeStruct((batch * t_len, width), y_dtype),
                   jax.ShapeDtypeStruct((batch, 1, width), F32),
                   jax.ShapeDtypeStruct((batch, CONV_WIDTH - 1, width), F32)),
        scratch_shapes=[pltpu.VMEM((tt + 2 * SUBLANES, cw), F32), pltpu.VMEM((1, cw), F32)],
        compiler_params=_cparams(("parallel", "parallel", "arbitrary")), name="rglru")(
            u_arr, g_arr, conv_w, conv_b, w_rg_a, b_rg_a, w_rg_x, b_rg_x, lam, h0, buf)


INFO_E1, INFO_E2, INFO_R1, INFO_R2, INFO_W1, INFO_W2 = range(6)


def _router_body(x_ref, w_ref, b_ref, info_ref, cnt_ref, run_ref, *, n_experts):
    tm = x_ref.shape[0]
    epg = n_experts // N_GROUPS

    @pl.when(pl.program_id(0) == 0)
    def _():
        run_ref[...] = jnp.zeros_like(run_ref)

    logits = jnp.dot(x_ref[...], w_ref[...], preferred_element_type=F32,
                     precision=lax.Precision.HIGHEST) + b_ref[...]
    lane_i = lax.broadcasted_iota(jnp.int32, (tm, LANES), 1)
    lane = lane_i.astype(F32)
    mx = jnp.max(logits, axis=-1, keepdims=True)
    e = jnp.exp(logits - mx)
    probs = e / jnp.sum(e, axis=-1, keepdims=True)

    def top1(vals):
        top = jnp.max(vals, axis=-1, keepdims=True)
        idx = jnp.min(jnp.where(vals == top, lane, float(LANES)), axis=-1, keepdims=True)
        return top, idx

    best = None
    for g in range(N_GROUPS):
        in_g = (lane_i >= g * epg) & (lane_i < (g + 1) * epg)
        pg = jnp.where(in_g, probs, -1.0)
        v1, i1 = top1(pg)
        v2, i2 = top1(jnp.where(lane == i1, -1.0, pg))
        score = v1 + v2
        if best is None:
            best = (score, v1, i1, v2, i2)
        else:
            better = score > best[0]
            best = tuple(jnp.where(better, new, old)
                         for new, old in zip((score, v1, i1, v2, i2), best))
    _, v1, i1, v2, i2 = best
    denom = v1 + v2
    w1 = v1 / denom
    w2 = v2 / denom

    hit1 = lane == i1
    hit2 = lane == i2
    cnt = jnp.where(hit1 | hit2, 1.0, 0.0)
    r = lax.broadcasted_iota(jnp.int32, (tm, tm), 0)
    c = lax.broadcasted_iota(jnp.int32, (tm, tm), 1)
    before = jnp.where(c < r, 1.0, 0.0).astype(BF16)
    prefix = jnp.dot(before, cnt.astype(BF16), preferred_element_type=F32) + run_ref[...]
    r1 = jnp.sum(jnp.where(hit1, prefix, 0.0), axis=-1, keepdims=True)
    r2 = jnp.sum(jnp.where(hit2, prefix, 0.0), axis=-1, keepdims=True)
    run_ref[...] = run_ref[...] + jnp.sum(cnt, axis=0, keepdims=True)

    info = jnp.zeros((tm, LANES), F32)
    for col, val in ((INFO_E1, i1), (INFO_E2, i2), (INFO_R1, r1), (INFO_R2, r2),
                     (INFO_W1, w1), (INFO_W2, w2)):
        info = jnp.where(lane_i == col, val, info)
    info_ref[...] = info
    cnt_ref[...] = run_ref[...]


def _router(x, w_pad, b_pad, n_experts):
    m, d = x.shape
    tm = _tile(m, TOK_TM, ROW_ALIGN)
    return pl.pallas_call(
        functools.partial(_router_body, n_experts=n_experts), grid=(m // tm,),
        in_specs=[pl.BlockSpec((tm, d), lambda i: (i, 0)),
                  pl.BlockSpec((d, LANES), lambda i: (0, 0)),
                  pl.BlockSpec((1, LANES), lambda i: (0, 0))],
        out_specs=[pl.BlockSpec((tm, LANES), lambda i: (i, 0)),
                   pl.BlockSpec((1, LANES), lambda i: (0, 0))],
        out_shape=(jax.ShapeDtypeStruct((m, LANES), F32), jax.ShapeDtypeStruct((1, LANES), F32)),
        scratch_shapes=[pltpu.VMEM((1, LANES), F32)],
        compiler_params=_cparams(("arbitrary",)), name="router")(x, w_pad, b_pad)


def _dispatch_body(pos_ref, x_ref, xs_in_hbm, xs_hbm, sem, *, tm):
    del xs_in_hbm
    base = pl.program_id(0) * tm

    def row_copy(n, slot):
        return pltpu.make_async_copy(x_ref.at[pl.ds(n, 1)],
                                     xs_hbm.at[pl.ds(pos_ref[TOP_K * (base + n) + slot], 1)], sem)

    def start(n, carry):
        for slot in range(TOP_K):
            row_copy(n, slot).start()
        return carry

    def wait(n, carry):
        for slot in range(TOP_K):
            row_copy(n, slot).wait()
        return carry

    lax.fori_loop(0, tm, start, 0)
    lax.fori_loop(0, tm, wait, 0)


def _dispatch(x, pos, n_rows):
    m, d = x.shape
    tm = _tile(m, TOK_TM, ROW_ALIGN)
    xs0 = jnp.zeros((n_rows, d), x.dtype)
    return pl.pallas_call(
        functools.partial(_dispatch_body, tm=tm),
        grid_spec=pltpu.PrefetchScalarGridSpec(
            num_scalar_prefetch=1, grid=(m // tm,),
            in_specs=[pl.BlockSpec((tm, d), lambda i, pos: (i, 0)),
                      pl.BlockSpec(memory_space=pl.ANY)],
            out_specs=pl.BlockSpec(memory_space=pl.ANY),
            scratch_shapes=[pltpu.SemaphoreType.DMA(())]),
        out_shape=jax.ShapeDtypeStruct((n_rows, d), x.dtype),
        input_output_aliases={2: 0},
        compiler_params=_cparams(("arbitrary",)), name="moe_dispatch")(pos, x, xs0)


def _expert_changed(te_ref):
    t = pl.program_id(0)
    return (t == 0) | (te_ref[t] != te_ref[jnp.maximum(t - 1, 0)])


def _ffn1_body(te_ref, nv_ref, x_ref, wg_ref, wu_ref, h_ref, wg_bf, wu_bf):
    @pl.when(_expert_changed(te_ref))
    def _():
        wg_bf[...] = wg_ref[...].astype(BF16)
        wu_bf[...] = wu_ref[...].astype(BF16)

    @pl.when(pl.program_id(0) < nv_ref[0])
    def _():
        xb = x_ref[...].astype(BF16)
        g = jnp.dot(xb, wg_bf[...], preferred_element_type=F32)
        u = jnp.dot(xb, wu_bf[...], preferred_element_type=F32)
        h_ref[...] = (g * jax.nn.sigmoid(g) * u).astype(h_ref.dtype)

    @pl.when(pl.program_id(0) >= nv_ref[0])
    def _():
        h_ref[...] = jnp.zeros_like(h_ref)


def _ffn2_body(te_ref, nv_ref, h_ref, wd_ref, y_ref, wd_bf):
    @pl.when(_expert_changed(te_ref))
    def _():
        wd_bf[...] = wd_ref[...].astype(BF16)

    @pl.when(pl.program_id(0) < nv_ref[0])
    def _():
        y_ref[...] = jnp.dot(h_ref[...], wd_bf[...], preferred_element_type=F32)

    @pl.when(pl.program_id(0) >= nv_ref[0])
    def _():
        y_ref[...] = jnp.zeros_like(y_ref)


def _expert_ffn(xs, tile_expert, n_valid, w_gate, w_up, w_down, layer):
    n_rows, d = xs.shape
    f = w_gate.shape[-1]
    n_tiles = n_rows // MOE_TM
    row = lambda width: pl.BlockSpec((MOE_TM, width), lambda t, te, nv: (t, 0))
    wspec = lambda a, b: pl.BlockSpec((None, None, a, b), lambda t, te, nv: (layer, te[t], 0, 0))
    hid = pl.pallas_call(
        _ffn1_body,
        grid_spec=pltpu.PrefetchScalarGridSpec(
            num_scalar_prefetch=2, grid=(n_tiles,),
            in_specs=[row(d), wspec(d, f), wspec(d, f)], out_specs=row(f),
            scratch_shapes=[pltpu.VMEM((d, f), BF16), pltpu.VMEM((d, f), BF16)]),
        out_shape=jax.ShapeDtypeStruct((n_rows, f), BF16),
        compiler_params=_cparams(("arbitrary",)), name="moe_ffn1")(
            tile_expert, n_valid, xs, w_gate, w_up)
    return pl.pallas_call(
        _ffn2_body,
        grid_spec=pltpu.PrefetchScalarGridSpec(
            num_scalar_prefetch=2, grid=(n_tiles,),
            in_specs=[row(f), wspec(f, d)], out_specs=row(d),
            scratch_shapes=[pltpu.VMEM((f, d), BF16)]),
        out_shape=jax.ShapeDtypeStruct((n_rows, d), F32),
        compiler_params=_cparams(("arbitrary",)), name="moe_ffn2")(
            tile_expert, n_valid, hid, w_down)


def _combine_body(pos_ref, x_ref, info_ref, g_ref, b_ref, y_hbm, o_ref, obf_ref, ybuf, sem,
                  *, tm, alpha):
    base = pl.program_id(0) * tm

    def row_copy(n, slot):
        return pltpu.make_async_copy(y_hbm.at[pl.ds(pos_ref[TOP_K * (base + n) + slot], 1)],
                                     ybuf.at[slot, pl.ds(n, 1)], sem)

    def start(n, carry):
        for slot in range(TOP_K):
            row_copy(n, slot).start()
        return carry

    def wait(n, carry):
        for slot in range(TOP_K):
            row_copy(n, slot).wait()
        return carry

    lax.fori_loop(0, tm, start, 0)
    lax.fori_loop(0, tm, wait, 0)
    info = info_ref[...]
    moe = info[:, INFO_W1:INFO_W1 + 1] * ybuf[0] + info[:, INFO_W2:INFO_W2 + 1] * ybuf[1]
    out = _layernorm_rows(alpha * x_ref[...] + moe, g_ref[...], b_ref[...])
    o_ref[...] = out
    obf_ref[...] = out.astype(BF16)


def _combine_ln(x, info, pos, y, g, b, layer, alpha):
    m, d = x.shape
    tm = _tile(m, TOK_TM, ROW_ALIGN)
    row = pl.BlockSpec((tm, d), lambda i, pos: (i, 0))
    vec = pl.BlockSpec((None, 1, d), lambda i, pos: (layer, 0, 0))
    return pl.pallas_call(
        functools.partial(_combine_body, tm=tm, alpha=alpha),
        grid_spec=pltpu.PrefetchScalarGridSpec(
            num_scalar_prefetch=1, grid=(m // tm,),
            in_specs=[row, pl.BlockSpec((tm, LANES), lambda i, pos: (i, 0)), vec, vec,
                      pl.BlockSpec(memory_space=pl.ANY)],
            out_specs=[row, row],
            scratch_shapes=[pltpu.VMEM((TOP_K, tm, d), F32), pltpu.SemaphoreType.DMA(())]),
        out_shape=(jax.ShapeDtypeStruct((m, d), F32), jax.ShapeDtypeStruct((m, d), BF16)),
        compiler_params=_cparams(("arbitrary",)), name="moe_combine_ln2")(pos, x, info, g, b, y)


def _moe_plan(info, counts, n_experts, n_tiles):
    e = info[:, INFO_E1:INFO_E2 + 1].astype(jnp.int32)
    rank = info[:, INFO_R1:INFO_R2 + 1].astype(jnp.int32)
    cnt = counts[0, :n_experts].astype(jnp.int32)
    tiles_per = (cnt + MOE_TM - 1) // MOE_TM
    tile_end = jnp.cumsum(tiles_per)
    tile_start = tile_end - tiles_per
    pos = (tile_start * MOE_TM)[e] + rank
    n_valid = tile_end[-1]
    t = jnp.minimum(jnp.arange(n_tiles, dtype=jnp.int32), n_valid - 1)
    tile_expert = jnp.sum(t[:, None] >= tile_end[None, :], axis=1).astype(jnp.int32)
    return pos.reshape(-1), tile_expert, n_valid.reshape(1).astype(jnp.int32)


def kernel(x_prompt, x_sample, cache_k_a, cache_v_a, cache_lf_a, cache_k_c, cache_v_c, state_h_b,
           state_conv_b, page_table, w_in, b_f, conv_w, conv_b, w_rg_a, b_rg_a, w_rg_x, b_rg_x, lam,
           w_proj_a, w_proj_b, w_proj_c, w_out, ln1_g, ln1_b, w_router, b_router, w_e_gate, w_e_up,
           w_e_down, ln2_g, ln2_b):
    batch, seq, d = x_prompt.shape
    dec_batch, t_new, _ = x_sample.shape
    depth = w_in.shape[0]
    d_a = w_proj_a.shape[1]
    w_b = w_proj_b.shape[1]
    d_c = w_proj_c.shape[1]
    h_a = b_f.shape[1]
    n_experts = w_router.shape[1]
    alpha = (2.0 * depth) ** 0.25
    scale = HEAD_DIM ** -0.5
    n_p = batch * seq
    n_s = dec_batch * t_new
    m_all = -(-(n_p + n_s) // ROW_ALIGN) * ROW_ALIGN
    assert d_a == d_c and d_a % MM_TN == 0 and n_p % SUBLANES == 0 and h_a <= SUBLANES

    o_f = 3 * d_a
    o_rest = o_f + h_a
    w_rest = _rebase_columns(w_in, o_rest)
    r_u, r_g = 0, w_b
    r_qc = 2 * w_b
    r_kvc = r_qc + d_c
    r_gl = r_qc + 3 * d_c
    bias_f = jnp.concatenate([b_f, jnp.zeros((depth, LANES - h_a), b_f.dtype)], axis=1)[:, None, :]

    vec3 = lambda a: a.reshape(depth, 1, a.shape[-1])
    conv_b3, lam3 = vec3(conv_b), vec3(lam)
    b_rg_a3 = b_rg_a.reshape(depth, 1, w_b)
    b_rg_x3 = b_rg_x.reshape(depth, 1, w_b)
    ln1_g3, ln1_b3, ln2_g3, ln2_b3 = vec3(ln1_g), vec3(ln1_b), vec3(ln2_g), vec3(ln2_b)
    w_r_pad = jnp.concatenate([w_router, jnp.zeros((d, LANES - n_experts), w_router.dtype)], axis=1)
    b_r_pad = jnp.concatenate([b_router, jnp.full((LANES - n_experts,), NEG_BIG, b_router.dtype)])[None, :]

    n_pool, page = cache_k_a.shape[1], cache_k_a.shape[2]
    rows_of = lambda c: c.reshape(depth, n_pool, page * c.shape[3], HEAD_DIM)
    ck_a, cv_a, ck_c, cv_c = rows_of(cache_k_a), rows_of(cache_v_a), rows_of(cache_k_c), rows_of(cache_v_c)
    clf_t = jnp.swapaxes(cache_lf_a, 2, 3)

    zeros_h = jnp.zeros((batch, 1, w_b), F32)
    zeros_buf = jnp.zeros((batch, CONV_WIDTH - 1, w_b), F32)
    n_tiles = (TOP_K * m_all + n_experts * (MOE_TM - 1) + MOE_TM - 1) // MOE_TM

    x = jnp.concatenate([x_prompt.reshape(n_p, d), x_sample.reshape(n_s, d),
                         jnp.zeros((m_all - n_p - n_s, d), F32)], axis=0)
    x_bf = x.astype(BF16)
    pad_rows = lambda a: jnp.concatenate(
        [a, jnp.zeros((m_all - a.shape[0], a.shape[1]), a.dtype)], axis=0)
    sample = lambda a: a[n_p:n_p + n_s]

    new_p = [[] for _ in range(7)]
    new_s = [[] for _ in range(7)]
    for l in range(depth):
        (qa16,) = _matmul(x_bf, w_in, l, 0, d_a, (BF16,), epilogue="scale", alpha=scale,
                          name="inproj_qa")
        kva32, kva16 = _matmul(x_bf, w_in, l, d_a, 2 * d_a, (F32, BF16), name="inproj_kva")
        (lf,) = _matmul(x_bf, w_in, l, o_f, LANES, (F32,), tn=LANES, epilogue="logsig_bias",
                        extra=bias_f, name="inproj_forget")
        (u32,) = _matmul(x_bf, w_rest, l, r_u, w_b, (F32,), tn=MM_TN_BF16, name="inproj_u")
        (g16,) = _matmul(x_bf, w_rest, l, r_g, w_b, (BF16,), tn=MM_TN_BF16, name="inproj_g")
        (qc16,) = _matmul(x_bf, w_rest, l, r_qc, d_c, (BF16,), tn=MM_TN_BF16, epilogue="scale",
                          alpha=scale, name="inproj_qc")
        kvc32, kvc16 = _matmul(x_bf, w_rest, l, r_kvc, 2 * d_c, (F32, BF16), tn=MM_TN_BF16,
                               name="inproj_kvc")
        (gates,) = _matmul(x_bf, w_rest, l, r_gl, 3 * d, (BF16,), tn=MM_TN_BF16,
                           epilogue="sigmoid", name="inproj_gates")

        c_col = _prompt_cumsum(lf, batch, seq)
        c_row = jnp.swapaxes(c_col.reshape(batch, seq, LANES)[:, :, :SUBLANES], 1, 2)
        ya_p = _fox_prompt(qa16, kva16, c_col, c_row, batch, seq, d_a)
        yc_p = _sb_prompt(qc16, kvc16, batch, seq, d_c)
        yb_p, hl_p, nb_p = _rglru(u32, g16, batch, seq, conv_w, conv_b3, w_rg_a, b_rg_a3,
                                  w_rg_x, b_rg_x3, lam3, zeros_h, zeros_buf, l, BF16)

        lf_s = sample(lf)[:, :SUBLANES].reshape(dec_batch, t_new, SUBLANES)
        nlf_t = jnp.concatenate([jnp.swapaxes(lf_s, 1, 2),
                                 jnp.zeros((dec_batch, SUBLANES, page - t_new), F32)], axis=2)
        ya_s, yc_s = _sample_attention(page_table, sample(qa16).astype(F32),
                                       sample(qc16).astype(F32), kva32, kvc32, n_p // t_new,
                                       nlf_t, ck_a, cv_a, clf_t, ck_c, cv_c, l, t_new, d_a)
        yb_s, hl_s, nb_s = _rglru(sample(u32), sample(g16).astype(F32), dec_batch, t_new,
                                  conv_w, conv_b3, w_rg_a, b_rg_a3, w_rg_x, b_rg_x3, lam3,
                                  state_h_b[l][:, None, :], state_conv_b[l], l, F32)

        ya = pad_rows(jnp.concatenate([ya_p, ya_s.astype(BF16)], axis=0))
        yb = pad_rows(jnp.concatenate([yb_p, yb_s.astype(BF16)], axis=0))
        yc = pad_rows(jnp.concatenate([yc_p, yc_s.astype(BF16)], axis=0))
        merged = _merge(ya, yb, yc, w_proj_a, w_proj_b, w_proj_c, gates, l)
        (pre1,) = _matmul(merged, w_out, l, 0, d, (F32,), epilogue="residual", extra=x,
                          alpha=alpha, name="out_proj")
        x1 = _layernorm(pre1, ln1_g3, ln1_b3, l)

        info, counts = _router(x1, w_r_pad, b_r_pad, n_experts)
        pos, tile_expert, n_valid = _moe_plan(info, counts, n_experts, n_tiles)
        xs = _dispatch(x1, pos, n_tiles * MOE_TM)
        y = _expert_ffn(xs, tile_expert, n_valid, w_e_gate, w_e_up, w_e_down, l)
        x, x_bf = _combine_ln(x1, info, pos, y, ln2_g3, ln2_b3, l, alpha)

        def heads(a, rows0, nrows, col0, lead):
            return a[rows0:rows0 + nrows, col0:col0 + d_a].reshape(lead + (d_a // HEAD_DIM, HEAD_DIM))

        for dst, rows0, nrows, lead in ((new_p, 0, n_p, (batch, seq)),
                                        (new_s, n_p, n_s, (dec_batch, t_new))):
            dst[0].append(heads(kva32, rows0, nrows, 0, lead))
            dst[1].append(heads(kva32, rows0, nrows, d_a, lead))
            dst[2].append(lf[rows0:rows0 + nrows, :h_a].reshape(lead + (h_a,)))
            dst[3].append(heads(kvc32, rows0, nrows, 0, lead))
            dst[4].append(heads(kvc32, rows0, nrows, d_c, lead))
        new_p[5].append(hl_p[:, 0, :])
        new_p[6].append(nb_p)
        new_s[5].append(hl_s[:, 0, :])
        new_s[6].append(nb_s)

    y_prompt = x[:n_p].reshape(batch, seq, d)
    y_sample = x[n_p:n_p + n_s].reshape(dec_batch, t_new, d)
    return (y_prompt, y_sample) + tuple(jnp.stack(s) for s in new_p) + tuple(jnp.stack(s) for s in new_s)
```
